```python
import jax, jax.numpy as jnp
from jax import lax
import numpy as np

D_MODEL = 1024
BATCH = 2
SEQ = 8192
DEPTH = 2
DEC_BATCH = 16
DEC_SEQ = 32
PAST_LEN = 4096

CHUNK = 64
N_A = DEPTH // 2
N_B = DEPTH - N_A
GLA_HEADS = 4
GLA_DK = D_MODEL // 2 // GLA_HEADS
GLA_DV = D_MODEL // GLA_HEADS
GLA_GATE_RANK = 16
GLA_TAU = 16.0
MLA_HEADS = 8
NOPE_DIM = 128
ROPE_DIM = 64
V_DIM = 128
Q_RANK = 512
KV_RANK = 256
ROPE_THETA = 10000.0
Q_BLOCK = 128
D_FF = 2816
CONV_W = 3
EPS = 1e-6

kernel_name = "yoco_gla_mla_convglu_stream_step"


def rmsnorm(x, g):
    xf = x.astype(jnp.float32)
    y = xf * lax.rsqrt(jnp.mean(xf * xf, axis=-1, keepdims=True) + EPS)
    return (y * g.astype(jnp.float32)).astype(x.dtype)


def rope(x, pos):
    half = ROPE_DIM // 2
    inv = ROPE_THETA ** (-jnp.arange(half, dtype=jnp.float32) * 2.0 / ROPE_DIM)
    ang = pos.astype(jnp.float32)[:, None] * inv[None, :]
    shp = (ang.shape[0],) + (1,) * (x.ndim - 3) + (half,)
    cos, sin = jnp.cos(ang).reshape(shp), jnp.sin(ang).reshape(shp)
    xf = x.astype(jnp.float32)
    x1, x2 = xf[..., :half], xf[..., half:]
    return jnp.concatenate([x1 * cos - x2 * sin, x1 * sin + x2 * cos], axis=-1).astype(x.dtype)


def causal_dwconv(u, prev, w, b):
    L = u.shape[1]
    full = jnp.concatenate([prev.astype(u.dtype), u], axis=1)
    out = b + sum(full[:, i:i + L] * w[i] for i in range(CONV_W))
    return out, full[:, -(CONV_W - 1):]


def conv_glu(h, prev, w_in, conv_w, conv_b, w_out):
    g, u = jnp.split(h @ w_in, [D_FF], axis=-1)
    gc, new_prev = causal_dwconv(g, prev, conv_w, conv_b)
    return (jax.nn.gelu(gc, approximate=False) * u) @ w_out, new_prev


def gla_chunk_step(S, inp):
    q, k, v, log_a = inp
    L = q.shape[2]
    b = jnp.cumsum(log_a, axis=2)
    causal = jnp.tril(jnp.ones((L, L), dtype=bool))[:, :, None]
    diff = b[:, :, :, None, :] - b[:, :, None, :, :]
    decay = jnp.exp(jnp.where(causal, diff, -jnp.inf))
    scores = jnp.sum(q[:, :, :, None, :] * k[:, :, None, :, :] * decay, axis=-1)
    o = jnp.einsum('bhij,bhjv->bhiv', scores, v) + jnp.einsum('bhid,bhdv->bhiv', q * jnp.exp(b), S)
    b_last = b[:, :, -1:, :]
    S_new = jnp.exp(b_last[:, :, 0, :])[..., None] * S + jnp.einsum('bhjd,bhjv->bhdv', k * jnp.exp(b_last - b), v)
    return S_new, o


def gla_mixer(h, S0, w_in, w_a1, w_a2, b_a, out_norm, w_out):
    B, L, _ = h.shape
    dk, dv = GLA_HEADS * GLA_DK, GLA_HEADS * GLA_DV
    q, k, v, r = jnp.split(h @ w_in, [dk, 2 * dk, 2 * dk + dv], axis=-1)
    log_a = jax.nn.log_sigmoid(((h @ w_a1) @ w_a2 + b_a).astype(jnp.float32)) / GLA_TAU

    def heads(t, d):
        return t.reshape(B, L, GLA_HEADS, d).transpose(0, 2, 1, 3).astype(jnp.float32)

    c = min(CHUNK, L)
    n = L // c

    def chunks(t):
        return t.reshape(B, GLA_HEADS, n, c, t.shape[-1]).transpose(2, 0, 1, 3, 4)

    xs = (chunks(heads(q, GLA_DK) * GLA_DK ** -0.5), chunks(heads(k, GLA_DK)),
          chunks(heads(v, GLA_DV)), chunks(heads(log_a, GLA_DK)))
    S, o = lax.scan(gla_chunk_step, S0.astype(jnp.float32), xs)
    o = o.transpose(1, 0, 3, 2, 4).reshape(B, L, GLA_HEADS, GLA_DV)
    o = rmsnorm(o, out_norm).reshape(B, L, dv).astype(h.dtype) * jax.nn.silu(r)
    return o @ w_out, S.astype(S0.dtype)


def mla_shared_kv(s, pos, past_ckv, past_kpe, kv_stream_norm, w_mla_dkv, mla_kv_norm,
                  mla_kp_norm, w_mla_ukv, mla_kn_norm):
    h = rmsnorm(s, kv_stream_norm)
    ckv, kpe = jnp.split(h @ w_mla_dkv, [KV_RANK], axis=-1)
    ckv = rmsnorm(ckv, mla_kv_norm)
    kpe = rope(rmsnorm(kpe, mla_kp_norm), pos)
    if past_ckv is None:
        all_ckv, all_kpe, k_pos = ckv, kpe, pos
    else:
        all_ckv = jnp.concatenate([past_ckv.astype(ckv.dtype), ckv], axis=1)
        all_kpe = jnp.concatenate([past_kpe.astype(kpe.dtype), kpe], axis=1)
        k_pos = jnp.arange(past_ckv.shape[1] + s.shape[1], dtype=jnp.int32)
    B, T, _ = all_ckv.shape
    kv = (all_ckv @ w_mla_ukv).reshape(B, T, MLA_HEADS, NOPE_DIM + V_DIM)
    k_nope = rmsnorm(kv[..., :NOPE_DIM], mla_kn_norm)
    v = kv[..., NOPE_DIM:]
    return ckv, kpe, k_nope, all_kpe, v, k_pos


def mla_attend(q_nope, q_pe, q_pos, k_nope, k_pe, v, k_pos):
    B, Lq = q_nope.shape[:2]
    scale = (NOPE_DIM + ROPE_DIM) ** -0.5
    k_chunk = k_pos // CHUNK

    def block(args):
        qn, qp, qpos = args
        s = (jnp.einsum('bqhd,bkhd->bhqk', qn, k_nope, preferred_element_type=jnp.float32)
             + jnp.einsum('bqhd,bkd->bhqk', qp, k_pe, preferred_element_type=jnp.float32)) * scale
        mask = k_chunk[None, :] <= (qpos // CHUNK)[:, None]
        p = jax.nn.softmax(jnp.where(mask, s, -jnp.inf), axis=-1)
        return jnp.einsum('bhqk,bkhd->bqhd', p.astype(v.dtype), v)

    if Lq > Q_BLOCK and Lq % Q_BLOCK == 0:
        n = Lq // Q_BLOCK

        def qb(t):
            return t.reshape((B, n, Q_BLOCK) + t.shape[2:]).swapaxes(0, 1)

        o = lax.map(block, (qb(q_nope), qb(q_pe), q_pos.reshape(n, Q_BLOCK)))
        return o.swapaxes(0, 1).reshape(B, Lq, MLA_HEADS, V_DIM)
    return block((q_nope, q_pe, q_pos))


def mla_layer(h, pos, k_nope, k_pe, v, k_pos, w_dq, q_norm, w_uq, qn_norm, qp_norm, w_out):
    B, L, _ = h.shape
    cq = rmsnorm(h @ w_dq, q_norm)
    q = (cq @ w_uq).reshape(B, L, MLA_HEADS, NOPE_DIM + ROPE_DIM)
    q_nope = rmsnorm(q[..., :NOPE_DIM], qn_norm)
    q_pe = rope(rmsnorm(q[..., NOPE_DIM:], qp_norm), pos)
    o = mla_attend(q_nope, q_pe, pos, k_nope, k_pe, v, k_pos)
    return o.reshape(B, L, MLA_HEADS * V_DIM) @ w_out


def run_trunk(x, pos, gla_S0, ffn_prev0, past_ckv, past_kpe,
              norm_mix, norm_ffn, w_ffn_in, ffn_conv_w, ffn_conv_b, w_ffn_out,
              w_gla_in, w_gla_a1, w_gla_a2, b_gla_a, gla_out_norm, w_gla_out,
              w_mla_dq, mla_q_norm, w_mla_uq, mla_qn_norm, mla_qp_norm, w_mla_out,
              kv_stream_norm, w_mla_dkv, mla_kv_norm, w_mla_ukv, mla_kn_norm, mla_kp_norm):
    new_gla, new_ffn = [], []
    new_ckv = new_kpe = None
    shared = None
    for l in range(DEPTH):
        h = rmsnorm(x, norm_mix[l])
        if l < N_A:
            o, S = gla_mixer(h, gla_S0[l], w_gla_in[l], w_gla_a1[l], w_gla_a2[l], b_gla_a[l],
                             gla_out_norm[l], w_gla_out[l])
            new_gla.append(S)
        else:
            if shared is None:
                new_ckv, new_kpe, k_nope, k_pe_all, v_all, k_pos = mla_shared_kv(
                    x, pos, past_ckv, past_kpe, kv_stream_norm, w_mla_dkv, mla_kv_norm,
                    mla_kp_norm, w_mla_ukv, mla_kn_norm)
                shared = (k_nope, k_pe_all, v_all, k_pos)
            j = l - N_A
            o = mla_layer(h, pos, shared[0], shared[1], shared[2], shared[3], w_mla_dq[j],
                          mla_q_norm[j], w_mla_uq[j], mla_qn_norm[j], mla_qp_norm[j], w_mla_out[j])
        x = x + o
        h = rmsnorm(x, norm_ffn[l])
        o, prev = conv_glu(h, ffn_prev0[l], w_ffn_in[l], ffn_conv_w[l], ffn_conv_b[l], w_ffn_out[l])
        new_ffn.append(prev)
        x = x + o
    return x, jnp.stack(new_gla), jnp.stack(new_ffn), new_ckv, new_kpe


def setup_inputs(seed: int = 0) -> dict:
    key = jax.random.key(seed)
    ks = iter(jax.random.split(key, 40))

    def nrm(shape, scale=1.0):
        return scale * jax.random.normal(next(ks), shape, jnp.float32)

    def gain(shape):
        return 1.0 + nrm(shape, 0.02)

    dk, dv = GLA_HEADS * GLA_DK, GLA_HEADS * GLA_DV
    return {
        "x_prompt": nrm((BATCH, SEQ, D_MODEL)),
        "x_sample": nrm((DEC_BATCH, DEC_SEQ, D_MODEL)),
        "state_gla": nrm((N_A, DEC_BATCH, GLA_HEADS, GLA_DK, GLA_DV)),
        "state_ffn_conv": nrm((DEPTH, DEC_BATCH, CONV_W - 1, D_FF)),
        "cache_mla_ckv": nrm((DEC_BATCH, PAST_LEN, KV_RANK)),
        "cache_mla_kpe": nrm((DEC_BATCH, PAST_LEN, ROPE_DIM)),
        "norm_mix": gain((DEPTH, D_MODEL)),
        "norm_ffn": gain((DEPTH, D_MODEL)),
        "w_ffn_in": nrm((DEPTH, D_MODEL, 2 * D_FF), D_MODEL ** -0.5),
        "ffn_conv_w": nrm((DEPTH, CONV_W, D_FF), CONV_W ** -0.5),
        "ffn_conv_b": nrm((DEPTH, D_FF), 0.02),
        "w_ffn_out": nrm((DEPTH, D_FF, D_MODEL), 0.5 * D_FF ** -0.5),
        "w_gla_in": nrm((N_A, D_MODEL, 2 * dk + 2 * dv), D_MODEL ** -0.5),
        "w_gla_a1": nrm((N_A, D_MODEL, GLA_GATE_RANK), D_MODEL ** -0.5),
        "w_gla_a2": nrm((N_A, GLA_GATE_RANK, dk), GLA_GATE_RANK ** -0.5),
        "b_gla_a": nrm((N_A, dk), 0.1),
        "gla_out_norm": gain((N_A, GLA_DV)),
        "w_gla_out": nrm((N_A, dv, D_MODEL), 0.5 * dv ** -0.5),
        "w_mla_dq": nrm((N_B, D_MODEL, Q_RANK), D_MODEL ** -0.5),
        "mla_q_norm": gain((N_B, Q_RANK)),
        "w_mla_uq": nrm((N_B, Q_RANK, MLA_HEADS * (NOPE_DIM + ROPE_DIM)), Q_RANK ** -0.5),
        "mla_qn_norm": gain((N_B, NOPE_DIM)),
        "mla_qp_norm": gain((N_B, ROPE_DIM)),
        "w_mla_out": nrm((N_B, MLA_HEADS * V_DIM, D_MODEL), 0.5 * (MLA_HEADS * V_DIM) ** -0.5),
        "kv_stream_norm": gain((D_MODEL,)),
        "w_mla_dkv": nrm((D_MODEL, KV_RANK + ROPE_DIM), D_MODEL ** -0.5),
        "mla_kv_norm": gain((KV_RANK,)),
        "w_mla_ukv": nrm((KV_RANK, MLA_HEADS * (NOPE_DIM + V_DIM)), KV_RANK ** -0.5),
        "mla_kn_norm": gain((NOPE_DIM,)),
        "mla_kp_norm": gain((ROPE_DIM,)),
    }


def reference(x_prompt, x_sample, state_gla, state_ffn_conv, cache_mla_ckv, cache_mla_kpe,
              norm_mix, norm_ffn, w_ffn_in, ffn_conv_w, ffn_conv_b, w_ffn_out,
              w_gla_in, w_gla_a1, w_gla_a2, b_gla_a, gla_out_norm, w_gla_out,
              w_mla_dq, mla_q_norm, w_mla_uq, mla_qn_norm, mla_qp_norm, w_mla_out,
              kv_stream_norm, w_mla_dkv, mla_kv_norm, w_mla_ukv, mla_kn_norm, mla_kp_norm):
    weights = (norm_mix, norm_ffn, w_ffn_in, ffn_conv_w, ffn_conv_b, w_ffn_out,
               w_gla_in, w_gla_a1, w_gla_a2, b_gla_a, gla_out_norm, w_gla_out,
               w_mla_dq, mla_q_norm, w_mla_uq, mla_qn_norm, mla_qp_norm, w_mla_out,
               kv_stream_norm, w_mla_dkv, mla_kv_norm, w_mla_ukv, mla_kn_norm, mla_kp_norm)
    bp, lp = x_prompt.shape[0], x_prompt.shape[1]
    p_pos = jnp.arange(lp, dtype=jnp.int32)
    gla0 = jnp.zeros((N_A, bp, GLA_HEADS, GLA_DK, GLA_DV), x_prompt.dtype)
    ffn0 = jnp.zeros((DEPTH, bp, CONV_W - 1, D_FF), x_prompt.dtype)
    y_prompt, gla_p, ffn_p, ckv_p, kpe_p = run_trunk(x_prompt, p_pos, gla0, ffn0, None, None, *weights)
    past = cache_mla_ckv.shape[1]
    s_pos = past + jnp.arange(x_sample.shape[1], dtype=jnp.int32)
    y_sample, gla_s, ffn_s, ckv_s, kpe_s = run_trunk(x_sample, s_pos, state_gla, state_ffn_conv,
                                                     cache_mla_ckv, cache_mla_kpe, *weights)
    return (y_prompt, y_sample, gla_p, ffn_p, ckv_p, kpe_p, gla_s, ffn_s, ckv_s, kpe_s)
```

```python
import functools

import jax
import jax.numpy as jnp
import numpy as np
from jax import lax
from jax.experimental import pallas as pl
from jax.experimental.pallas import tpu as pltpu

D_MODEL = 1024
CHUNK = 64
GLA_HEADS = 4
GLA_DK = 128
GLA_DV = 256
GLA_GATE_RANK = 16
GLA_TAU = 16.0
MLA_HEADS = 8
NOPE_DIM = 128
ROPE_DIM = 64
V_DIM = 128
Q_RANK = 512
KV_RANK = 256
ROPE_THETA = 10000.0
D_FF = 2816
CONV_W = 3
EPS = 1e-6

LANES = 128
GLA_SUB = 16
QK_PAD = 256
MASK_VALUE = -1e30
VMEM_LIMIT = 56 * 1024 * 1024

F32 = jnp.float32
BF16 = jnp.bfloat16


def _dot(a, b):
    return jnp.dot(a, b, preferred_element_type=F32)


def _dot_nt(a, b):
    return lax.dot_general(a, b, (((1,), (1,)), ((), ())), preferred_element_type=F32)


def _dot_tn(a, b):
    return lax.dot_general(a, b, (((0,), (0,)), ((), ())), preferred_element_type=F32)


def _rms(x, g):
    return x * lax.rsqrt(jnp.mean(x * x, axis=-1, keepdims=True) + EPS) * g


def _const_spec(shape):
    zeros = (0,) * len(shape)
    return pl.BlockSpec(shape, lambda *_: zeros)


def _params(semantics):
    return pltpu.CompilerParams(dimension_semantics=semantics, vmem_limit_bytes=VMEM_LIMIT)


def _split3(x):
    hi = x.astype(BF16)
    r1 = x - hi.astype(F32)
    mid = r1.astype(BF16)
    lo = (r1 - mid.astype(F32)).astype(BF16)
    return hi, mid, lo


def _gla_intra_scores(q, k, b, q_c, k_c, b_c, ks, chunk):
    sub = GLA_SUB
    nblk = chunk // sub
    rowb = lax.broadcasted_iota(jnp.int32, (chunk, GLA_DK), 0) // sub
    scores = None
    if nblk > 1:
        q_parts, k_parts = [], []
        for blk in range(1, nblk):
            rho = b_c[blk * sub - 1:blk * sub, ks]
            qd = jnp.where(rowb == blk, q * jnp.exp(jnp.minimum(b - rho, 0.0)), 0.0)
            kd = jnp.where(rowb < blk, k * jnp.exp(jnp.minimum(rho - b, 0.0)), 0.0)
            q_parts.append(qd.astype(BF16))
            k_parts.append(kd.astype(BF16))
        scores = _dot_nt(jnp.concatenate(q_parts, axis=1), jnp.concatenate(k_parts, axis=1))
    lane = lax.broadcasted_iota(jnp.int32, (sub, chunk), 1)
    blocks = []
    for blk in range(nblk):
        rows = slice(blk * sub, (blk + 1) * sub)
        b_blk = b_c[rows, ks]
        q_blk = q_c[rows, ks]
        a_blk = jnp.zeros((sub, chunk), F32)
        for jj in range(sub):
            row_j = slice(blk * sub + jj, blk * sub + jj + 1)
            t = jnp.exp(jnp.minimum(b_blk - b_c[row_j, ks], 0.0)) * (q_blk * k_c[row_j, ks])
            a_blk = jnp.where(lane == blk * sub + jj, jnp.sum(t, axis=-1, keepdims=True), a_blk)
        blocks.append(a_blk)
    diag = jnp.concatenate(blocks, axis=0) if nblk > 1 else blocks[0]
    row = lax.broadcasted_iota(jnp.int32, (chunk, chunk), 0)
    col = lax.broadcasted_iota(jnp.int32, (chunk, chunk), 1)
    diag = jnp.where(col <= row, diag, 0.0)
    return diag if scores is None else scores + diag


def _gla_kernel(x_ref, s0_ref, g_ref, win_ref, wa2_ref, ba_ref, on_ref, wout_ref,
                y_ref, s_ref, q_s, k_s, v_s, r_s, b_s, o_s, st_s, q_c, k_c, b_c,
                *, nb, tb, chunk):
    step = pl.program_id(1)
    heads, dk, dv = GLA_HEADS, GLA_DK, GLA_DV
    m = nb * tb
    units_per_seq = tb // chunk

    @pl.when(step == 0)
    def _load_state():
        for n in range(nb):
            for hd in range(heads):
                st_s[n, hd] = s0_ref[n, hd].T

    x = x_ref[...].reshape(m, D_MODEL)
    h = _rms(x, g_ref[...]).astype(BF16)
    z = _dot(h, win_ref[...])
    nk = heads * dk
    nv = heads * dv
    q_s[...] = z[:, :nk] * (dk ** -0.5)
    k_s[...] = z[:, nk:2 * nk]
    v_s[...] = z[:, 2 * nk:2 * nk + nv].astype(BF16)
    r_s[...] = z[:, 2 * nk + nv:2 * nk + 2 * nv]
    gate_lo = z[:, 2 * nk + 2 * nv:].astype(BF16)
    xg = _dot(gate_lo, wa2_ref[...]) + ba_ref[...]
    log_a = (jnp.minimum(xg, 0.0) - jnp.log1p(jnp.exp(-jnp.abs(xg)))) * (1.0 / GLA_TAU)

    tri_r = lax.broadcasted_iota(jnp.int32, (chunk, chunk), 0)
    tri_c = lax.broadcasted_iota(jnp.int32, (chunk, chunk), 1)
    tril = (tri_c <= tri_r).astype(BF16)
    for u in range(m // chunk):
        hi, mid, lo = _split3(log_a[u * chunk:(u + 1) * chunk])
        b_s[u * chunk:(u + 1) * chunk, :] = _dot(tril, hi) + _dot(tril, mid) + _dot(tril, lo)

    def unit_body(u, carry):
        r0 = pl.multiple_of(u * chunk, chunk)
        n = u // units_per_seq
        rows = pl.ds(r0, chunk)
        q_c[...] = q_s[rows, :]
        k_c[...] = k_s[rows, :]
        b_c[...] = b_s[rows, :]
        for hd in range(heads):
            ks = slice(hd * dk, (hd + 1) * dk)
            vs = slice(hd * dv, (hd + 1) * dv)
            q = q_c[:, ks]
            k = k_c[:, ks]
            b = b_c[:, ks]
            v = v_s[rows, vs]
            st = st_s[n, hd]
            b_last = b_c[chunk - 1:chunk, ks]
            o = _dot_nt((q * jnp.exp(b)).astype(BF16), st.astype(BF16))
            scores = _gla_intra_scores(q, k, b, q_c, k_c, b_c, ks, chunk)
            o = o + _dot(scores.astype(BF16), v)
            k_dec = (k * jnp.exp(b_last - b)).astype(BF16)
            st_s[n, hd] = st * jnp.exp(b_last) + _dot_tn(v, k_dec)
            o = _rms(o, on_ref[...])
            r = r_s[rows, vs]
            o_s[rows, vs] = (o * (r * jax.nn.sigmoid(r))).astype(BF16)
        return carry

    lax.fori_loop(0, m // chunk, unit_body, 0)

    y = _dot(o_s[...], wout_ref[...]) + x
    y_ref[...] = y.reshape(nb, tb, D_MODEL)

    @pl.when(step == pl.num_programs(1) - 1)
    def _store_state():
        for n in range(nb):
            for hd in range(heads):
                s_ref[n, hd] = st_s[n, hd].T


def _gla_layer(x, s0, w, *, nb, tb, chunk):
    bsz, seq, _ = x.shape
    m = nb * tb
    nk, nv = GLA_HEADS * GLA_DK, GLA_HEADS * GLA_DV
    kern = functools.partial(_gla_kernel, nb=nb, tb=tb, chunk=chunk)
    state_spec = pl.BlockSpec((nb, GLA_HEADS, GLA_DK, GLA_DV), lambda b, i: (b, 0, 0, 0))
    x_spec = pl.BlockSpec((nb, tb, D_MODEL), lambda b, i: (b, i, 0))
    return pl.pallas_call(
        kern,
        grid=(bsz // nb, seq // tb),
        in_specs=[x_spec, state_spec,
                  _const_spec((1, D_MODEL)), _const_spec(w["gla_in"].shape),
                  _const_spec(w["gla_a2"].shape), _const_spec((1, nk)),
                  _const_spec((1, GLA_DV)), _const_spec(w["gla_out"].shape)],
        out_specs=[x_spec, state_spec],
        out_shape=[jax.ShapeDtypeStruct(x.shape, F32), jax.ShapeDtypeStruct(s0.shape, F32)],
        scratch_shapes=[pltpu.VMEM((m, nk), F32), pltpu.VMEM((m, nk), F32),
                        pltpu.VMEM((m, nv), BF16), pltpu.VMEM((m, nv), F32),
                        pltpu.VMEM((m, nk), F32), pltpu.VMEM((m, nv), BF16),
                        pltpu.VMEM((nb, GLA_HEADS, GLA_DV, GLA_DK), F32),
                        pltpu.VMEM((chunk, nk), F32), pltpu.VMEM((chunk, nk), F32),
                        pltpu.VMEM((chunk, nk), F32)],
        compiler_params=_params(("arbitrary", "arbitrary")),
        name="gla_layer",
    )(x, s0, w["norm_mix0"], w["gla_in"], w["gla_a2"], w["gla_ba"], w["gla_on"], w["gla_out"])


def _ffn_kernel(x_ref, prev_ref, g_ref, win_ref, cw_ref, cb_ref, wout_ref,
                y_ref, tail_ref, carry_s, *, nb, tb):
    step = pl.program_id(1)
    m = nb * tb

    @pl.when(step == 0)
    def _load_carry():
        carry_s[...] = prev_ref[...]

    x = x_ref[...].reshape(m, D_MODEL)
    h = _rms(x, g_ref[...]).astype(BF16)
    z = _dot(h, win_ref[...])
    gate = z[:, :D_FF]
    up = z[:, D_FF:]
    row = lax.broadcasted_iota(jnp.int32, (tb, D_FF), 0)
    w0, w1, w2 = cw_ref[0:1, :], cw_ref[1:2, :], cw_ref[2:3, :]
    convs = []
    for n in range(nb):
        g_n = gate[n * tb:(n + 1) * tb]
        c0 = carry_s[n, 0:1, :]
        c1 = carry_s[n, 1:2, :]
        back1 = jnp.where(row == 0, c1, pltpu.roll(g_n, 1, 0))
        back2 = jnp.where(row == 0, c0, jnp.where(row == 1, c1, pltpu.roll(g_n, 2, 0)))
        convs.append(cb_ref[...] + ((back2 * w0 + back1 * w1) + g_n * w2))
        carry_s[n] = g_n[tb - (CONV_W - 1):tb]
    conv = jnp.concatenate(convs, axis=0) if nb > 1 else convs[0]
    gelu = 0.5 * conv * (1.0 + lax.erf(conv * (0.5 ** 0.5)))
    act = (gelu * up).astype(BF16)
    y = _dot(act, wout_ref[...]) + x
    y_ref[...] = y.reshape(nb, tb, D_MODEL)
    tail_ref[...] = carry_s[...]


def _ffn_layer(x, prev, w, layer, *, nb, tb):
    bsz, seq, _ = x.shape
    kern = functools.partial(_ffn_kernel, nb=nb, tb=tb)
    x_spec = pl.BlockSpec((nb, tb, D_MODEL), lambda b, i: (b, i, 0))
    tail_spec = pl.BlockSpec((nb, CONV_W - 1, D_FF), lambda b, i: (b, 0, 0))
    return pl.pallas_call(
        kern,
        grid=(bsz // nb, seq // tb),
        in_specs=[x_spec, tail_spec, _const_spec((1, D_MODEL)),
                  _const_spec((D_MODEL, 2 * D_FF)), _const_spec((CONV_W, D_FF)),
                  _const_spec((1, D_FF)), _const_spec((D_FF, D_MODEL))],
        out_specs=[x_spec, tail_spec],
        out_shape=[jax.ShapeDtypeStruct(x.shape, F32), jax.ShapeDtypeStruct(prev.shape, F32)],
        scratch_shapes=[pltpu.VMEM((nb, CONV_W - 1, D_FF), F32)],
        compiler_params=_params(("arbitrary", "arbitrary")),
        name=f"conv_glu_{layer}",
    )(x, prev, w["norm_ffn"][layer], w["ffn_in"][layer], w["ffn_cw"][layer],
      w["ffn_cb"][layer], w["ffn_out"][layer])


def _rope_pair(a, gain, cs):
    rs = lax.rsqrt(jnp.sum(a * a, axis=-1, keepdims=True) * (1.0 / LANES) + EPS)
    t = a * rs * gain * cs
    return t + pltpu.roll(t, ROPE_DIM, 1)


def _mla_prep_kernel(x_ref, cs_ref, gkv_ref, gq_ref, wdkv_ref, kvn_ref, kpg_ref,
                     wdq_ref, qn_ref, wuq_ref, qnn_ref, qpg_ref,
                     ckv_ref, kpe_ref, q_ref):
    x = x_ref[...]
    xn = x * lax.rsqrt(jnp.mean(x * x, axis=-1, keepdims=True) + EPS)
    cs = cs_ref[...]
    zk = _dot((xn * gkv_ref[...]).astype(BF16), wdkv_ref[...])
    ckv_ref[...] = _rms(zk[:, :KV_RANK], kvn_ref[...])
    kpe_ref[...] = _rope_pair(zk[:, KV_RANK:], kpg_ref[...], cs)[:, :ROPE_DIM]

    cq = _rms(_dot((xn * gq_ref[...]).astype(BF16), wdq_ref[...]), qn_ref[...]).astype(BF16)
    zq = _dot(cq, wuq_ref[...])
    scale = (NOPE_DIM + ROPE_DIM) ** -0.5
    for hd in range(MLA_HEADS):
        lo = hd * QK_PAD
        q_nope = _rms(zq[:, lo:lo + NOPE_DIM], qnn_ref[...])
        q_pe = _rope_pair(zq[:, lo + NOPE_DIM:lo + QK_PAD], qpg_ref[...], cs)
        q_ref[:, lo:lo + NOPE_DIM] = (q_nope * scale).astype(BF16)
        q_ref[:, lo + NOPE_DIM:lo + QK_PAD] = (q_pe * scale).astype(BF16)


def _mla_prep(x2d, cs, w, *, tm, cs_blocks):
    rows = x2d.shape[0]
    row_spec = lambda width: pl.BlockSpec((tm, width), lambda r: (r, 0))
    return pl.pallas_call(
        _mla_prep_kernel,
        grid=(rows // tm,),
        in_specs=[row_spec(D_MODEL), pl.BlockSpec((tm, LANES), lambda r: (r % cs_blocks, 0)),
                  _const_spec((1, D_MODEL)), _const_spec((1, D_MODEL)),
                  _const_spec(w["mla_dkv"].shape), _const_spec((1, KV_RANK)), _const_spec((1, LANES)),
                  _const_spec(w["mla_dq"].shape), _const_spec((1, Q_RANK)),
                  _const_spec(w["mla_uq"].shape), _const_spec((1, NOPE_DIM)), _const_spec((1, LANES))],
        out_specs=[row_spec(KV_RANK), row_spec(ROPE_DIM), row_spec(MLA_HEADS * QK_PAD)],
        out_shape=[jax.ShapeDtypeStruct((rows, KV_RANK), F32),
                   jax.ShapeDtypeStruct((rows, ROPE_DIM), F32),
                   jax.ShapeDtypeStruct((rows, MLA_HEADS * QK_PAD), BF16)],
        compiler_params=_params(("arbitrary",)),
        name="mla_prep",
    )(x2d, cs, w["kv_stream_norm"], w["norm_mix1"], w["mla_dkv"], w["mla_kvn"], w["mla_kpg"],
      w["mla_dq"], w["mla_qn"], w["mla_uq"], w["mla_qnn"], w["mla_qpg"])


def _upproj_kernel(ckv_ref, kpe_ref, wukv_ref, knn_ref, k_ref, v_ref):
    z = _dot(ckv_ref[...].astype(BF16), wukv_ref[...])
    kpe = kpe_ref[...].astype(BF16)
    nope_all = MLA_HEADS * NOPE_DIM
    for hd in range(MLA_HEADS):
        lo = hd * QK_PAD
        k_ref[:, lo:lo + NOPE_DIM] = _rms(z[:, hd * NOPE_DIM:(hd + 1) * NOPE_DIM],
                                          knn_ref[...]).astype(BF16)
        k_ref[:, lo + NOPE_DIM:lo + QK_PAD] = kpe
    v_ref[...] = z[:, nope_all:].astype(BF16)


def _mla_upproj(ckv2d, kpe2d, w, *, tm):
    rows = ckv2d.shape[0]
    row_spec = lambda width: pl.BlockSpec((tm, width), lambda r: (r, 0))
    return pl.pallas_call(
        _upproj_kernel,
        grid=(rows // tm,),
        in_specs=[row_spec(KV_RANK), row_spec(LANES),
                  _const_spec(w["mla_ukv"].shape), _const_spec((1, NOPE_DIM))],
        out_specs=[row_spec(MLA_HEADS * QK_PAD), row_spec(MLA_HEADS * V_DIM)],
        out_shape=[jax.ShapeDtypeStruct((rows, MLA_HEADS * QK_PAD), BF16),
                   jax.ShapeDtypeStruct((rows, MLA_HEADS * V_DIM), BF16)],
        compiler_params=_params(("arbitrary",)),
        name="mla_upproj",
    )(ckv2d, kpe2d, w["mla_ukv"], w["mla_knn"])


def _last_kv_block(qi, *, tq, tk, q_off, t_valid):
    last_q = q_off + (qi + 1) * tq - 1
    visible = jnp.minimum(t_valid, (last_q // CHUNK + 1) * CHUNK)
    return (visible - 1) // tk


def _attn_kernel(q_ref, k_ref, v_ref, x_ref, wo_ref, y_ref, m_s, l_s, acc_s, o_s,
                 *, tq, tk, q_off, t_valid):
    qi = pl.program_id(1)
    ki = pl.program_id(2)

    @pl.when(ki == 0)
    def _init():
        m_s[...] = jnp.full(m_s.shape, MASK_VALUE, F32)
        l_s[...] = jnp.zeros(l_s.shape, F32)
        acc_s[...] = jnp.zeros(acc_s.shape, F32)

    @pl.when(ki <= _last_kv_block(qi, tq=tq, tk=tk, q_off=q_off, t_valid=t_valid))
    def _step():
        q_pos = q_off + qi * tq + lax.broadcasted_iota(jnp.int32, (tq, tk), 0)
        k_pos = ki * tk + lax.broadcasted_iota(jnp.int32, (tq, tk), 1)
        shift = CHUNK.bit_length() - 1
        visible = ((k_pos >> shift) <= (q_pos >> shift)) & (k_pos < t_valid)
        for hd in range(MLA_HEADS):
            qk = slice(hd * QK_PAD, (hd + 1) * QK_PAD)
            s = _dot_nt(q_ref[:, qk], k_ref[:, qk])
            s = jnp.where(visible, s, MASK_VALUE)
            m_old = m_s[hd]
            m_new = jnp.maximum(m_old, jnp.max(s, axis=-1, keepdims=True))
            p = jnp.exp(s - m_new)
            alpha = jnp.exp(m_old - m_new)
            l_s[hd] = alpha * l_s[hd] + jnp.sum(p, axis=-1, keepdims=True)
            acc_s[hd] = alpha * acc_s[hd] + _dot(p.astype(BF16),
                                                 v_ref[:, hd * V_DIM:(hd + 1) * V_DIM])
            m_s[hd] = m_new

    @pl.when(ki == pl.num_programs(2) - 1)
    def _finish():
        for hd in range(MLA_HEADS):
            o_s[:, hd * V_DIM:(hd + 1) * V_DIM] = (acc_s[hd] / l_s[hd]).astype(BF16)
        y_ref[...] = _dot(o_s[...], wo_ref[...]) + x_ref[...]


def _mla_attention(q, k, v, x, w_out, *, tq, tk, q_off, t_valid):
    bsz, seq, _ = x.shape
    t_pad = k.shape[1]
    last = functools.partial(_last_kv_block, tq=tq, tk=tk, q_off=q_off, t_valid=t_valid)
    n_kv = (min(t_valid, ((q_off + seq - 1) // CHUNK + 1) * CHUNK) - 1) // tk + 1
    assert n_kv * tk <= t_pad
    kern = functools.partial(_attn_kernel, tq=tq, tk=tk, q_off=q_off, t_valid=t_valid)
    q_spec = lambda width: pl.BlockSpec((None, tq, width), lambda b, i, j: (b, i, 0))
    kv_spec = lambda width: pl.BlockSpec((None, tk, width),
                                         lambda b, i, j: (b, jnp.minimum(j, last(i)), 0))
    return pl.pallas_call(
        kern,
        grid=(bsz, seq // tq, n_kv),
        in_specs=[q_spec(MLA_HEADS * QK_PAD), kv_spec(MLA_HEADS * QK_PAD),
                  kv_spec(MLA_HEADS * V_DIM), q_spec(D_MODEL), _const_spec(w_out.shape)],
        out_specs=q_spec(D_MODEL),
        out_shape=jax.ShapeDtypeStruct(x.shape, F32),
        scratch_shapes=[pltpu.VMEM((MLA_HEADS, tq, 1), F32), pltpu.VMEM((MLA_HEADS, tq, 1), F32),
                        pltpu.VMEM((MLA_HEADS, tq, V_DIM), F32),
                        pltpu.VMEM((tq, MLA_HEADS * V_DIM), BF16)],
        compiler_params=_params(("arbitrary", "arbitrary", "arbitrary")),
        name="mla_attention",
    )(q, k, v, x, w_out)


def _rotate_half_cols(w):
    half = ROPE_DIM // 2
    return jnp.concatenate([-w[..., half:], w[..., :half]], axis=-1)


def _swap_halves(g):
    half = ROPE_DIM // 2
    return jnp.concatenate([g[..., half:], g[..., :half]], axis=-1)


def _prepare_weights(norm_mix, norm_ffn, w_ffn_in, ffn_conv_w, ffn_conv_b, w_ffn_out,
                     w_gla_in, w_gla_a1, w_gla_a2, b_gla_a, gla_out_norm, w_gla_out,
                     w_mla_dq, mla_q_norm, w_mla_uq, mla_qn_norm, mla_qp_norm, w_mla_out,
                     kv_stream_norm, w_mla_dkv, mla_kv_norm, w_mla_ukv, mla_kn_norm, mla_kp_norm):
    row = lambda g: g.reshape(1, -1).astype(F32)
    pad_rank = LANES - GLA_GATE_RANK
    gla_in = jnp.concatenate([w_gla_in[0], jnp.pad(w_gla_a1[0], ((0, 0), (0, pad_rank)))], axis=1)
    gla_a2 = jnp.pad(w_gla_a2[0], ((0, pad_rank), (0, 0)))

    dkv_pe = w_mla_dkv[:, KV_RANK:]
    mla_dkv = jnp.concatenate([w_mla_dkv[:, :KV_RANK], dkv_pe, _rotate_half_cols(dkv_pe)], axis=1)

    uq = w_mla_uq[0].reshape(Q_RANK, MLA_HEADS, NOPE_DIM + ROPE_DIM)
    uq_pe = uq[..., NOPE_DIM:]
    mla_uq = jnp.concatenate([uq[..., :NOPE_DIM], uq_pe, _rotate_half_cols(uq_pe)], axis=-1)
    mla_uq = mla_uq.reshape(Q_RANK, MLA_HEADS * QK_PAD)

    ukv = w_mla_ukv.reshape(KV_RANK, MLA_HEADS, NOPE_DIM + V_DIM)
    mla_ukv = jnp.concatenate([ukv[..., :NOPE_DIM].reshape(KV_RANK, -1),
                               ukv[..., NOPE_DIM:].reshape(KV_RANK, -1)], axis=1)

    pair_gain = lambda g: row(jnp.concatenate([g, _swap_halves(g)], axis=-1))
    return {
        "norm_mix0": row(norm_mix[0]), "norm_mix1": row(norm_mix[1]),
        "norm_ffn": [row(norm_ffn[l]) for l in range(2)],
        "ffn_in": [w_ffn_in[l].astype(BF16) for l in range(2)],
        "ffn_cw": [ffn_conv_w[l] for l in range(2)],
        "ffn_cb": [row(ffn_conv_b[l]) for l in range(2)],
        "ffn_out": [w_ffn_out[l].astype(BF16) for l in range(2)],
        "gla_in": gla_in.astype(BF16), "gla_a2": gla_a2.astype(BF16), "gla_ba": row(b_gla_a[0]),
        "gla_on": row(gla_out_norm[0]), "gla_out": w_gla_out[0].astype(BF16),
        "kv_stream_norm": row(kv_stream_norm), "mla_dkv": mla_dkv.astype(BF16),
        "mla_kvn": row(mla_kv_norm), "mla_kpg": pair_gain(mla_kp_norm),
        "mla_dq": w_mla_dq[0].astype(BF16), "mla_qn": row(mla_q_norm[0]),
        "mla_uq": mla_uq.astype(BF16), "mla_qnn": row(mla_qn_norm[0]),
        "mla_qpg": pair_gain(mla_qp_norm[0]),
        "mla_ukv": mla_ukv.astype(BF16), "mla_knn": row(mla_kn_norm),
        "mla_out": w_mla_out[0].astype(BF16),
    }


def _rope_table(pos0, seq):
    half = ROPE_DIM // 2
    inv = ROPE_THETA ** (-jnp.arange(half, dtype=F32) * 2.0 / ROPE_DIM)
    ang = (pos0 + jnp.arange(seq, dtype=jnp.int32)).astype(F32)[:, None] * inv[None, :]
    cos, sin = jnp.cos(ang), jnp.sin(ang)
    return jnp.concatenate([cos, cos, sin, sin], axis=-1)


def _tiles(bsz, seq):
    if seq <= 256:
        return dict(nb=bsz, tb=seq, tm=bsz * seq, tq=seq)
    return dict(nb=1, tb=256, tm=512, tq=512)


def _run_trunk(x, pos0, gla_s0, ffn_prev, past_ckv, past_kpe, w):
    bsz, seq, _ = x.shape
    plan = _tiles(bsz, seq)
    nb, tb, tm, tq = plan["nb"], plan["tb"], plan["tm"], plan["tq"]
    chunk = min(CHUNK, seq)

    x1, gla_s = _gla_layer(x, gla_s0, w, nb=nb, tb=tb, chunk=chunk)
    x2, tail0 = _ffn_layer(x1, ffn_prev[0], w, 0, nb=nb, tb=tb)

    cs = _rope_table(pos0, seq)
    if tm > seq:
        cs = jnp.tile(cs, (tm // seq, 1))
    ckv, kpe, q = _mla_prep(x2.reshape(bsz * seq, D_MODEL), cs, w, tm=tm,
                            cs_blocks=cs.shape[0] // tm)
    ckv = ckv.reshape(bsz, seq, KV_RANK)
    kpe = kpe.reshape(bsz, seq, ROPE_DIM)
    q = q.reshape(bsz, seq, MLA_HEADS * QK_PAD)

    if past_ckv is None:
        all_ckv, all_kpe = ckv, kpe
    else:
        all_ckv = jnp.concatenate([past_ckv, ckv], axis=1)
        all_kpe = jnp.concatenate([past_kpe, kpe], axis=1)
    t_valid = all_ckv.shape[1]
    tk = 512
    t_pad = -(-t_valid // tk) * tk
    all_ckv = jnp.pad(all_ckv, ((0, 0), (0, t_pad - t_valid), (0, 0)))
    all_kpe = jnp.pad(all_kpe, ((0, 0), (0, t_pad - t_valid), (0, LANES - ROPE_DIM)))
    k, v = _mla_upproj(all_ckv.reshape(bsz * t_pad, KV_RANK), all_kpe.reshape(bsz * t_pad, LANES),
                       w, tm=512)
    k = k.reshape(bsz, t_pad, MLA_HEADS * QK_PAD)
    v = v.reshape(bsz, t_pad, MLA_HEADS * V_DIM)
    x3 = _mla_attention(q, k, v, x2, w["mla_out"], tq=tq, tk=tk, q_off=pos0, t_valid=t_valid)

    x4, tail1 = _ffn_layer(x3, ffn_prev[1], w, 1, nb=nb, tb=tb)
    return x4, gla_s[None], jnp.stack([tail0, tail1]), ckv, kpe


def kernel(x_prompt, x_sample, state_gla, state_ffn_conv, cache_mla_ckv, cache_mla_kpe, norm_mix, norm_ffn, w_ffn_in, ffn_conv_w, ffn_conv_b, w_ffn_out, w_gla_in, w_gla_a1, w_gla_a2, b_gla_a, gla_out_norm, w_gla_out, w_mla_dq, mla_q_norm, w_mla_uq, mla_qn_norm, mla_qp_norm, w_mla_out, kv_stream_norm, w_mla_dkv, mla_kv_norm, w_mla_ukv, mla_kn_norm, mla_kp_norm):
    assert norm_mix.shape[0] == 2 and w_gla_in.shape[0] == 1 and w_mla_dq.shape[0] == 1
    w = _prepare_weights(norm_mix, norm_ffn, w_ffn_in, ffn_conv_w, ffn_conv_b, w_ffn_out,
                         w_gla_in, w_gla_a1, w_gla_a2, b_gla_a, gla_out_norm, w_gla_out,
                         w_mla_dq, mla_q_norm, w_mla_uq, mla_qn_norm, mla_qp_norm, w_mla_out,
                         kv_stream_norm, w_mla_dkv, mla_kv_norm, w_mla_ukv, mla_kn_norm, mla_kp_norm)
    bp = x_prompt.shape[0]
    gla0 = jnp.zeros((bp, GLA_HEADS, GLA_DK, GLA_DV), F32)
    ffn0 = jnp.zeros((2, bp, CONV_W - 1, D_FF), F32)
    y_p, gla_p, ffn_p, ckv_p, kpe_p = _run_trunk(x_prompt, 0, gla0, ffn0, None, None, w)
    past = cache_mla_ckv.shape[1]
    y_s, gla_s, ffn_s, ckv_s, kpe_s = _run_trunk(x_sample, past, state_gla[0], state_ffn_conv,
                                                 cache_mla_ckv, cache_mla_kpe, w)
    return (y_p, y_s, gla_p, ffn_p, ckv_p, kpe_p, gla_s, ffn_s, ckv_s, kpe_s)
```

```python
import functools

import jax
import jax.numpy as jnp
import numpy as np
from jax import lax
from jax.experimental import pallas as pl
from jax.experimental.pallas import tpu as pltpu

D_MODEL = 1024
CHUNK = 64
GLA_HEADS = 4
GLA_DK = 128
GLA_DV = 256
GLA_GATE_RANK = 16
GLA_TAU = 16.0
MLA_HEADS = 8
NOPE_DIM = 128
ROPE_DIM = 64
V_DIM = 128
Q_RANK = 512
KV_RANK = 256
ROPE_THETA = 10000.0
D_FF = 2816
CONV_W = 3
EPS = 1e-6

LANES = 128
GLA_SUB = 16
QK_PAD = 256
MASK_VALUE = -1e30
VMEM_LIMIT = 56 * 1024 * 1024

F32 = jnp.float32
BF16 = jnp.bfloat16


def _dot(a, b):
    return jnp.dot(a, b, preferred_element_type=F32)


def _dot_nt(a, b):
    return lax.dot_general(a, b, (((1,), (1,)), ((), ())), preferred_element_type=F32)


def _dot_tn(a, b):
    return lax.dot_general(a, b, (((0,), (0,)), ((), ())), preferred_element_type=F32)


def _rms(x, g):
    return x * lax.rsqrt(jnp.mean(x * x, axis=-1, keepdims=True) + EPS) * g


def _const_spec(shape):
    zeros = (0,) * len(shape)
    return pl.BlockSpec(shape, lambda *_: zeros)


def _params(semantics):
    return pltpu.CompilerParams(dimension_semantics=semantics, vmem_limit_bytes=VMEM_LIMIT)


def _split3(x):
    hi = x.astype(BF16)
    r1 = x - hi.astype(F32)
    mid = r1.astype(BF16)
    lo = (r1 - mid.astype(F32)).astype(BF16)
    return hi, mid, lo


def _gla_intra_scores(q, k, b, q_c, k_c, b_c, ks, chunk):
    sub = GLA_SUB
    nblk = chunk // sub
    rowb = lax.broadcasted_iota(jnp.int32, (chunk, GLA_DK), 0) // sub
    scores = None
    if nblk > 1:
        q_parts, k_parts = [], []
        for blk in range(1, nblk):
            rho = b_c[blk * sub - 1:blk * sub, ks]
            qd = jnp.where(rowb == blk, q * jnp.exp(jnp.minimum(b - rho, 0.0)), 0.0)
            kd = jnp.where(rowb < blk, k * jnp.exp(jnp.minimum(rho - b, 0.0)), 0.0)
            q_parts.append(qd.astype(BF16))
            k_parts.append(kd.astype(BF16))
        scores = _dot_nt(jnp.concatenate(q_parts, axis=1), jnp.concatenate(k_parts, axis=1))
    lane = lax.broadcasted_iota(jnp.int32, (sub, chunk), 1)
    blocks = []
    for blk in range(nblk):
        rows = slice(blk * sub, (blk + 1) * sub)
        b_blk = b_c[rows, ks]
        q_blk = q_c[rows, ks]
        a_blk = jnp.zeros((sub, chunk), F32)
        for jj in range(sub):
            row_j = slice(blk * sub + jj, blk * sub + jj + 1)
            t = jnp.exp(jnp.minimum(b_blk - b_c[row_j, ks], 0.0)) * (q_blk * k_c[row_j, ks])
            a_blk = jnp.where(lane == blk * sub + jj, jnp.sum(t, axis=-1, keepdims=True), a_blk)
        blocks.append(a_blk)
    diag = jnp.concatenate(blocks, axis=0) if nblk > 1 else blocks[0]
    row = lax.broadcasted_iota(jnp.int32, (chunk, chunk), 0)
    col = lax.broadcasted_iota(jnp.int32, (chunk, chunk), 1)
    diag = jnp.where(col <= row, diag, 0.0)
    return diag if scores is None else scores + diag


def _gla_kernel(x_ref, s0_ref, g_ref, win_ref, wa2_ref, ba_ref, on_ref, wout_ref,
                y_ref, s_ref, q_s, k_s, v_s, r_s, b_s, o_s, st_s, q_c, k_c, b_c,
                *, nb, tb, chunk):
    step = pl.program_id(1)
    heads, dk, dv = GLA_HEADS, GLA_DK, GLA_DV
    m = nb * tb
    units_per_seq = tb // chunk

    @pl.when(step == 0)
    def _load_state():
        for n in range(nb):
            for hd in range(heads):
                st_s[n, hd] = s0_ref[n, hd].T

    x = x_ref[...].reshape(m, D_MODEL)
    h = _rms(x, g_ref[...]).astype(BF16)
    z = _dot(h, win_ref[...])
    nk = heads * dk
    nv = heads * dv
    q_s[...] = z[:, :nk] * (dk ** -0.5)
    k_s[...] = z[:, nk:2 * nk]
    v_s[...] = z[:, 2 * nk:2 * nk + nv].astype(BF16)
    r_s[...] = z[:, 2 * nk + nv:2 * nk + 2 * nv]
    gate_lo = z[:, 2 * nk + 2 * nv:].astype(BF16)
    xg = _dot(gate_lo, wa2_ref[...]) + ba_ref[...]
    log_a = (jnp.minimum(xg, 0.0) - jnp.log1p(jnp.exp(-jnp.abs(xg)))) * (1.0 / GLA_TAU)

    tri_r = lax.broadcasted_iota(jnp.int32, (chunk, chunk), 0)
    tri_c = lax.broadcasted_iota(jnp.int32, (chunk, chunk), 1)
    tril = (tri_c <= tri_r).astype(BF16)
    for u in range(m // chunk):
        hi, mid, lo = _split3(log_a[u * chunk:(u + 1) * chunk])
        b_s[u * chunk:(u + 1) * chunk, :] = _dot(tril, hi) + _dot(tril, mid) + _dot(tril, lo)

    def unit_body(u, carry):
        r0 = pl.multiple_of(u * chunk, chunk)
        n = u // units_per_seq
        rows = pl.ds(r0, chunk)
        q_c[...] = q_s[rows, :]
        k_c[...] = k_s[rows, :]
        b_c[...] = b_s[rows, :]
        for hd in range(heads):
            ks = slice(hd * dk, (hd + 1) * dk)
            vs = slice(hd * dv, (hd + 1) * dv)
            q = q_c[:, ks]
            k = k_c[:, ks]
            b = b_c[:, ks]
            v = v_s[rows, vs]
            st = st_s[n, hd]
            b_last = b_c[chunk - 1:chunk, ks]
            o = _dot_nt((q * jnp.exp(b)).astype(BF16), st.astype(BF16))
            scores = _gla_intra_scores(q, k, b, q_c, k_c, b_c, ks, chunk)
            o = o + _dot(scores.astype(BF16), v)
            k_dec = (k * jnp.exp(b_last - b)).astype(BF16)
            st_s[n, hd] = st * jnp.exp(b_last) + _dot_tn(v, k_dec)
            o = _rms(o, on_ref[...])
            r = r_s[rows, vs]
            o_s[rows, vs] = (o * (r * jax.nn.sigmoid(r))).astype(BF16)
        return carry

    lax.fori_loop(0, m // chunk, unit_body, 0)

    y = _dot(o_s[...], wout_ref[...]) + x
    y_ref[...] = y.reshape(nb, tb, D_MODEL)

    @pl.when(step == pl.num_programs(1) - 1)
    def _store_state():
        for n in range(nb):
            for hd in range(heads):
                s_ref[n, hd] = st_s[n, hd].T


def _gla_layer(x, s0, w, *, nb, tb, chunk):
    bsz, seq, _ = x.shape
    m = nb * tb
    nk, nv = GLA_HEADS * GLA_DK, GLA_HEADS * GLA_DV
    kern = functools.partial(_gla_kernel, nb=nb, tb=tb, chunk=chunk)
    state_spec = pl.BlockSpec((nb, GLA_HEADS, GLA_DK, GLA_DV), lambda b, i: (b, 0, 0, 0))
    x_spec = pl.BlockSpec((nb, tb, D_MODEL), lambda b, i: (b, i, 0))
    return pl.pallas_call(
        kern,
        grid=(bsz // nb, seq // tb),
        in_specs=[x_spec, state_spec,
                  _const_spec((1, D_MODEL)), _const_spec(w["gla_in"].shape),
                  _const_spec(w["gla_a2"].shape), _const_spec((1, nk)),
                  _const_spec((1, GLA_DV)), _const_spec(w["gla_out"].shape)],
        out_specs=[x_spec, state_spec],
        out_shape=[jax.ShapeDtypeStruct(x.shape, F32), jax.ShapeDtypeStruct(s0.shape, F32)],
        scratch_shapes=[pltpu.VMEM((m, nk), F32), pltpu.VMEM((m, nk), F32),
                        pltpu.VMEM((m, nv), BF16), pltpu.VMEM((m, nv), F32),
                        pltpu.VMEM((m, nk), F32), pltpu.VMEM((m, nv), BF16),
                        pltpu.VMEM((nb, GLA_HEADS, GLA_DV, GLA_DK), F32),
                        pltpu.VMEM((chunk, nk), F32), pltpu.VMEM((chunk, nk), F32),
                        pltpu.VMEM((chunk, nk), F32)],
        compiler_params=_params(("arbitrary", "arbitrary")),
        name="gla_layer",
    )(x, s0, w["norm_mix0"], w["gla_in"], w["gla_a2"], w["gla_ba"], w["gla_on"], w["gla_out"])


def _ffn_kernel(x_ref, prev_ref, g_ref, win_ref, cw_ref, cb_ref, wout_ref,
                y_ref, tail_ref, carry_s, *, nb, tb):
    step = pl.program_id(1)
    m = nb * tb

    @pl.when(step == 0)
    def _load_carry():
        carry_s[...] = prev_ref[...]

    x = x_ref[...].reshape(m, D_MODEL)
    h = _rms(x, g_ref[...]).astype(BF16)
    z = _dot(h, win_ref[...])
    gate = z[:, :D_FF]
    up = z[:, D_FF:]
    row = lax.broadcasted_iota(jnp.int32, (tb, D_FF), 0)
    w0, w1, w2 = cw_ref[0:1, :], cw_ref[1:2, :], cw_ref[2:3, :]
    convs = []
    for n in range(nb):
        g_n = gate[n * tb:(n + 1) * tb]
        c0 = carry_s[n, 0:1, :]
        c1 = carry_s[n, 1:2, :]
        back1 = jnp.where(row == 0, c1, pltpu.roll(g_n, 1, 0))
        back2 = jnp.where(row == 0, c0, jnp.where(row == 1, c1, pltpu.roll(g_n, 2, 0)))
        convs.append(cb_ref[...] + ((back2 * w0 + back1 * w1) + g_n * w2))
        carry_s[n] = g_n[tb - (CONV_W - 1):tb]
    conv = jnp.concatenate(convs, axis=0) if nb > 1 else convs[0]
    gelu = 0.5 * conv * (1.0 + lax.erf(conv * (0.5 ** 0.5)))
    act = (gelu * up).astype(BF16)
    y = _dot(act, wout_ref[...]) + x
    y_ref[...] = y.reshape(nb, tb, D_MODEL)
    tail_ref[...] = carry_s[...]


def _ffn_layer(x, prev, w, layer, *, nb, tb):
    bsz, seq, _ = x.shape
    kern = functools.partial(_ffn_kernel, nb=nb, tb=tb)
    x_spec = pl.BlockSpec((nb, tb, D_MODEL), lambda b, i: (b, i, 0))
    tail_spec = pl.BlockSpec((nb, CONV_W - 1, D_FF), lambda b, i: (b, 0, 0))
    return pl.pallas_call(
        kern,
        grid=(bsz // nb, seq // tb),
        in_specs=[x_spec, tail_spec, _const_spec((1, D_MODEL)),
                  _const_spec((D_MODEL, 2 * D_FF)), _const_spec((CONV_W, D_FF)),
                  _const_spec((1, D_FF)), _const_spec((D_FF, D_MODEL))],
        out_specs=[x_spec, tail_spec],
        out_shape=[jax.ShapeDtypeStruct(x.shape, F32), jax.ShapeDtypeStruct(prev.shape, F32)],
        scratch_shapes=[pltpu.VMEM((nb, CONV_W - 1, D_FF), F32)],
        compiler_params=_params(("arbitrary", "arbitrary")),
        name=f"conv_glu_{layer}",
    )(x, prev, w["norm_ffn"][layer], w["ffn_in"][layer], w["ffn_cw"][layer],
      w["ffn_cb"][layer], w["ffn_out"][layer])


def _rope_pair(a, gain, cs):
    rs = lax.rsqrt(jnp.sum(a * a, axis=-1, keepdims=True) * (1.0 / LANES) + EPS)
    t = a * rs * gain * cs
    return t + pltpu.roll(t, ROPE_DIM, 1)


def _mla_prep_kernel(x_ref, cs_ref, gkv_ref, gq_ref, wdkv_ref, kvn_ref, kpg_ref,
                     wdq_ref, qn_ref, wuq_ref, qnn_ref, qpg_ref,
                     ckv_ref, kpe_ref, q_ref):
    x = x_ref[...]
    xn = x * lax.rsqrt(jnp.mean(x * x, axis=-1, keepdims=True) + EPS)
    cs = cs_ref[...]
    zk = _dot((xn * gkv_ref[...]).astype(BF16), wdkv_ref[...])
    ckv_ref[...] = _rms(zk[:, :KV_RANK], kvn_ref[...])
    kpe_ref[...] = _rope_pair(zk[:, KV_RANK:], kpg_ref[...], cs)[:, :ROPE_DIM]

    cq = _rms(_dot((xn * gq_ref[...]).astype(BF16), wdq_ref[...]), qn_ref[...]).astype(BF16)
    zq = _dot(cq, wuq_ref[...])
    scale = (NOPE_DIM + ROPE_DIM) ** -0.5 * float(np.log2(np.e))
    for hd in range(MLA_HEADS):
        lo = hd * QK_PAD
        q_nope = _rms(zq[:, lo:lo + NOPE_DIM], qnn_ref[...])
        q_pe = _rope_pair(zq[:, lo + NOPE_DIM:lo + QK_PAD], qpg_ref[...], cs)
        q_ref[:, lo:lo + NOPE_DIM] = (q_nope * scale).astype(BF16)
        q_ref[:, lo + NOPE_DIM:lo + QK_PAD] = (q_pe * scale).astype(BF16)


def _mla_prep(x2d, cs, w, *, tm, cs_blocks):
    rows = x2d.shape[0]
    row_spec = lambda width: pl.BlockSpec((tm, width), lambda r: (r, 0))
    return pl.pallas_call(
        _mla_prep_kernel,
        grid=(rows // tm,),
        in_specs=[row_spec(D_MODEL), pl.BlockSpec((tm, LANES), lambda r: (r % cs_blocks, 0)),
                  _const_spec((1, D_MODEL)), _const_spec((1, D_MODEL)),
                  _const_spec(w["mla_dkv"].shape), _const_spec((1, KV_RANK)), _const_spec((1, LANES)),
                  _const_spec(w["mla_dq"].shape), _const_spec((1, Q_RANK)),
                  _const_spec(w["mla_uq"].shape), _const_spec((1, NOPE_DIM)), _const_spec((1, LANES))],
        out_specs=[row_spec(KV_RANK), row_spec(ROPE_DIM), row_spec(MLA_HEADS * QK_PAD)],
        out_shape=[jax.ShapeDtypeStruct((rows, KV_RANK), F32),
                   jax.ShapeDtypeStruct((rows, ROPE_DIM), F32),
                   jax.ShapeDtypeStruct((rows, MLA_HEADS * QK_PAD), BF16)],
        compiler_params=_params(("arbitrary",)),
        name="mla_prep",
    )(x2d, cs, w["kv_stream_norm"], w["norm_mix1"], w["mla_dkv"], w["mla_kvn"], w["mla_kpg"],
      w["mla_dq"], w["mla_qn"], w["mla_uq"], w["mla_qnn"], w["mla_qpg"])


def _upproj_kernel(ckv_ref, kpe_ref, wuk_ref, wuvt_ref, knn_ref, k_ref, vt_ref):
    ckv = ckv_ref[...].astype(BF16)
    z = _dot(ckv, wuk_ref[...])
    kpe = kpe_ref[...].astype(BF16)
    for hd in range(MLA_HEADS):
        lo = hd * QK_PAD
        k_ref[:, lo:lo + NOPE_DIM] = _rms(z[:, hd * NOPE_DIM:(hd + 1) * NOPE_DIM],
                                          knn_ref[...]).astype(BF16)
        k_ref[:, lo + NOPE_DIM:lo + QK_PAD] = kpe
    vt_ref[...] = _dot_nt(wuvt_ref[...], ckv).astype(BF16)


def _mla_upproj(ckv, kpe, w, *, tm):
    bsz, t_pad, _ = ckv.shape
    row_spec = lambda width: pl.BlockSpec((None, tm, width), lambda b, t: (b, t, 0))
    return pl.pallas_call(
        _upproj_kernel,
        grid=(bsz, t_pad // tm),
        in_specs=[row_spec(KV_RANK), row_spec(LANES), _const_spec(w["mla_uk"].shape),
                  _const_spec(w["mla_uvt"].shape), _const_spec((1, NOPE_DIM))],
        out_specs=[row_spec(MLA_HEADS * QK_PAD),
                   pl.BlockSpec((None, MLA_HEADS * V_DIM, tm), lambda b, t: (b, 0, t))],
        out_shape=[jax.ShapeDtypeStruct((bsz, t_pad, MLA_HEADS * QK_PAD), BF16),
                   jax.ShapeDtypeStruct((bsz, MLA_HEADS * V_DIM, t_pad), BF16)],
        compiler_params=_params(("arbitrary", "arbitrary")),
        name="mla_upproj",
    )(ckv, kpe, w["mla_uk"], w["mla_uvt"], w["mla_knn"])


def _last_kv_block(qi, *, tq, tk, q_off, t_valid):
    last_q = q_off + (qi + 1) * tq - 1
    visible = jnp.minimum(t_valid, (last_q // CHUNK + 1) * CHUNK)
    return (visible - 1) // tk


def _attn_kernel(q_ref, k_ref, vt_ref, x_ref, wo_ref, y_ref, m_s, l_s, acc_s, o_s,
                 *, tq, tk, q_off, t_valid):
    qi = pl.program_id(1)
    ki = pl.program_id(2)

    @pl.when(ki == 0)
    def _init():
        m_s[...] = jnp.full(m_s.shape, MASK_VALUE, F32)
        l_s[...] = jnp.zeros(l_s.shape, F32)
        acc_s[...] = jnp.zeros(acc_s.shape, F32)

    def scores(hd):
        qk = slice(hd * QK_PAD, (hd + 1) * QK_PAD)
        return _dot_nt(k_ref[:, qk], q_ref[:, qk])

    def step(masked):
        if masked:
            k_pos = ki * tk + lax.broadcasted_iota(jnp.int32, (tk, tq), 0)
            q_pos = q_off + qi * tq + lax.broadcasted_iota(jnp.int32, (tk, tq), 1)
            shift = CHUNK.bit_length() - 1
            visible = ((k_pos >> shift) <= (q_pos >> shift)) & (k_pos < t_valid)
        s_next = scores(0)
        for hd in range(MLA_HEADS):
            row = slice(hd, hd + 1)
            s = s_next
            if hd + 1 < MLA_HEADS:
                s_next = scores(hd + 1)
            if masked:
                s = jnp.where(visible, s, MASK_VALUE)
            m_old = m_s[row, :]
            m_new = jnp.maximum(m_old, jnp.max(s, axis=0, keepdims=True))
            p = jnp.exp2(s - m_new)
            alpha = jnp.exp2(m_old - m_new)
            l_s[row, :] = alpha * l_s[row, :] + jnp.sum(p, axis=0, keepdims=True)
            acc_s[hd] = alpha * acc_s[hd] + _dot(vt_ref[hd * V_DIM:(hd + 1) * V_DIM, :],
                                                 p.astype(BF16))
            m_s[row, :] = m_new

    first_q = q_off + qi * tq
    block_end = (ki + 1) * tk
    needs_mask = (block_end > (first_q // CHUNK + 1) * CHUNK) | (block_end > t_valid)
    in_range = ki <= _last_kv_block(qi, tq=tq, tk=tk, q_off=q_off, t_valid=t_valid)
    pl.when(in_range & needs_mask)(functools.partial(step, True))
    pl.when(in_range & jnp.logical_not(needs_mask))(functools.partial(step, False))

    @pl.when(ki == pl.num_programs(2) - 1)
    def _finish():
        for hd in range(MLA_HEADS):
            o_s[hd * V_DIM:(hd + 1) * V_DIM, :] = (acc_s[hd] / l_s[hd:hd + 1, :]).astype(BF16)
        y_ref[...] = _dot_tn(o_s[...], wo_ref[...]) + x_ref[...]


def _mla_attention(q, k, vt, x, w_out, *, tq, tk, q_off, t_valid):
    bsz, seq, _ = x.shape
    t_pad = k.shape[1]
    last = functools.partial(_last_kv_block, tq=tq, tk=tk, q_off=q_off, t_valid=t_valid)
    n_kv = (min(t_valid, ((q_off + seq - 1) // CHUNK + 1) * CHUNK) - 1) // tk + 1
    assert n_kv * tk <= t_pad
    kern = functools.partial(_attn_kernel, tq=tq, tk=tk, q_off=q_off, t_valid=t_valid)
    q_spec = lambda width: pl.BlockSpec((None, tq, width), lambda b, i, j: (b, i, 0))
    kv_block = lambda i, j: jnp.minimum(j, last(i))
    return pl.pallas_call(
        kern,
        grid=(bsz, seq // tq, n_kv),
        in_specs=[q_spec(MLA_HEADS * QK_PAD),
                  pl.BlockSpec((None, tk, MLA_HEADS * QK_PAD), lambda b, i, j: (b, kv_block(i, j), 0)),
                  pl.BlockSpec((None, MLA_HEADS * V_DIM, tk), lambda b, i, j: (b, 0, kv_block(i, j))),
                  q_spec(D_MODEL), _const_spec(w_out.shape)],
        out_specs=q_spec(D_MODEL),
        out_shape=jax.ShapeDtypeStruct(x.shape, F32),
        scratch_shapes=[pltpu.VMEM((MLA_HEADS, tq), F32), pltpu.VMEM((MLA_HEADS, tq), F32),
                        pltpu.VMEM((MLA_HEADS, V_DIM, tq), F32),
                        pltpu.VMEM((MLA_HEADS * V_DIM, tq), BF16)],
        compiler_params=_params(("arbitrary", "arbitrary", "arbitrary")),
        name="mla_attention",
    )(q, k, vt, x, w_out)


def _rotate_half_cols(w):
    half = ROPE_DIM // 2
    return jnp.concatenate([-w[..., half:], w[..., :half]], axis=-1)


def _swap_halves(g):
    half = ROPE_DIM // 2
    return jnp.concatenate([g[..., half:], g[..., :half]], axis=-1)


def _prepare_weights(norm_mix, norm_ffn, w_ffn_in, ffn_conv_w, ffn_conv_b, w_ffn_out,
                     w_gla_in, w_gla_a1, w_gla_a2, b_gla_a, gla_out_norm, w_gla_out,
                     w_mla_dq, mla_q_norm, w_mla_uq, mla_qn_norm, mla_qp_norm, w_mla_out,
                     kv_stream_norm, w_mla_dkv, mla_kv_norm, w_mla_ukv, mla_kn_norm, mla_kp_norm):
    row = lambda g: g.reshape(1, -1).astype(F32)
    pad_rank = LANES - GLA_GATE_RANK
    gla_in = jnp.concatenate([w_gla_in[0], jnp.pad(w_gla_a1[0], ((0, 0), (0, pad_rank)))], axis=1)
    gla_a2 = jnp.pad(w_gla_a2[0], ((0, pad_rank), (0, 0)))

    dkv_pe = w_mla_dkv[:, KV_RANK:]
    mla_dkv = jnp.concatenate([w_mla_dkv[:, :KV_RANK], dkv_pe, _rotate_half_cols(dkv_pe)], axis=1)

    uq = w_mla_uq[0].reshape(Q_RANK, MLA_HEADS, NOPE_DIM + ROPE_DIM)
    uq_pe = uq[..., NOPE_DIM:]
    mla_uq = jnp.concatenate([uq[..., :NOPE_DIM], uq_pe, _rotate_half_cols(uq_pe)], axis=-1)
    mla_uq = mla_uq.reshape(Q_RANK, MLA_HEADS * QK_PAD)

    ukv = w_mla_ukv.reshape(KV_RANK, MLA_HEADS, NOPE_DIM + V_DIM)
    mla_uk = ukv[..., :NOPE_DIM].reshape(KV_RANK, MLA_HEADS * NOPE_DIM)
    mla_uvt = ukv[..., NOPE_DIM:].reshape(KV_RANK, MLA_HEADS * V_DIM).T

    pair_gain = lambda g: row(jnp.concatenate([g, _swap_halves(g)], axis=-1))
    return {
        "norm_mix0": row(norm_mix[0]), "norm_mix1": row(norm_mix[1]),
        "norm_ffn": [row(norm_ffn[l]) for l in range(2)],
        "ffn_in": [w_ffn_in[l].astype(BF16) for l in range(2)],
        "ffn_cw": [ffn_conv_w[l] for l in range(2)],
        "ffn_cb": [row(ffn_conv_b[l]) for l in range(2)],
        "ffn_out": [w_ffn_out[l].astype(BF16) for l in range(2)],
        "gla_in": gla_in.astype(BF16), "gla_a2": gla_a2.astype(BF16), "gla_ba": row(b_gla_a[0]),
        "gla_on": row(gla_out_norm[0]), "gla_out": w_gla_out[0].astype(BF16),
        "kv_stream_norm": row(kv_stream_norm), "mla_dkv": mla_dkv.astype(BF16),
        "mla_kvn": row(mla_kv_norm), "mla_kpg": pair_gain(mla_kp_norm),
        "mla_dq": w_mla_dq[0].astype(BF16), "mla_qn": row(mla_q_norm[0]),
        "mla_uq": mla_uq.astype(BF16), "mla_qnn": row(mla_qn_norm[0]),
        "mla_qpg": pair_gain(mla_qp_norm[0]),
        "mla_uk": mla_uk.astype(BF16), "mla_uvt": mla_uvt.astype(BF16),
        "mla_knn": row(mla_kn_norm),
        "mla_out": w_mla_out[0].astype(BF16),
    }


def _rope_table(pos0, seq):
    half = ROPE_DIM // 2
    inv = ROPE_THETA ** (-jnp.arange(half, dtype=F32) * 2.0 / ROPE_DIM)
    ang = (pos0 + jnp.arange(seq, dtype=jnp.int32)).astype(F32)[:, None] * inv[None, :]
    cos, sin = jnp.cos(ang), jnp.sin(ang)
    return jnp.concatenate([cos, cos, sin, sin], axis=-1)


def _tiles(bsz, seq):
    if seq <= 256:
        return dict(nb=bsz, tb=seq, tm=bsz * seq, tq=seq)
    return dict(nb=1, tb=256, tm=512, tq=512)


def _run_trunk(x, pos0, gla_s0, ffn_prev, past_ckv, past_kpe, w):
    bsz, seq, _ = x.shape
    plan = _tiles(bsz, seq)
    nb, tb, tm, tq = plan["nb"], plan["tb"], plan["tm"], plan["tq"]
    chunk = min(CHUNK, seq)

    x1, gla_s = _gla_layer(x, gla_s0, w, nb=nb, tb=tb, chunk=chunk)
    x2, tail0 = _ffn_layer(x1, ffn_prev[0], w, 0, nb=nb, tb=tb)

    cs = _rope_table(pos0, seq)
    if tm > seq:
        cs = jnp.tile(cs, (tm // seq, 1))
    ckv, kpe, q = _mla_prep(x2.reshape(bsz * seq, D_MODEL), cs, w, tm=tm,
                            cs_blocks=cs.shape[0] // tm)
    ckv = ckv.reshape(bsz, seq, KV_RANK)
    kpe = kpe.reshape(bsz, seq, ROPE_DIM)
    q = q.reshape(bsz, seq, MLA_HEADS * QK_PAD)

    if past_ckv is None:
        all_ckv, all_kpe = ckv, kpe
    else:
        all_ckv = jnp.concatenate([past_ckv, ckv], axis=1)
        all_kpe = jnp.concatenate([past_kpe, kpe], axis=1)
    t_valid = all_ckv.shape[1]
    tk = 512
    t_pad = -(-t_valid // tk) * tk
    all_ckv = jnp.pad(all_ckv, ((0, 0), (0, t_pad - t_valid), (0, 0)))
    all_kpe = jnp.pad(all_kpe, ((0, 0), (0, t_pad - t_valid), (0, LANES - ROPE_DIM)))
    k, vt = _mla_upproj(all_ckv, all_kpe, w, tm=tk)
    x3 = _mla_attention(q, k, vt, x2, w["mla_out"], tq=tq, tk=tk, q_off=pos0, t_valid=t_valid)

    x4, tail1 = _ffn_layer(x3, ffn_prev[1], w, 1, nb=nb, tb=tb)
    return x4, gla_s[None], jnp.stack([tail0, tail1]), ckv, kpe


def kernel(x_prompt, x_sample, state_gla, state_ffn_conv, cache_mla_ckv, cache_mla_kpe, norm_mix, norm_ffn, w_ffn_in, ffn_conv_w, ffn_conv_b, w_ffn_out, w_gla_in, w_gla_a1, w_gla_a2, b_gla_a, gla_out_norm, w_gla_out, w_mla_dq, mla_q_norm, w_mla_uq, mla_qn_norm, mla_qp_norm, w_mla_out, kv_stream_norm, w_mla_dkv, mla_kv_norm, w_mla_ukv, mla_kn_norm, mla_kp_norm):
    assert norm_mix.shape[0] == 2 and w_gla_in.shape[0] == 1 and w_mla_dq.shape[0] == 1
    w = _prepare_weights(norm_mix, norm_ffn, w_ffn_in, ffn_conv_w, ffn_conv_b, w_ffn_out,
                         w_gla_in, w_gla_a1, w_gla_a2, b_gla_a, gla_out_norm, w_gla_out,
                         w_mla_dq, mla_q_norm, w_mla_uq, mla_qn_norm, mla_qp_norm, w_mla_out,
                         kv_stream_norm, w_mla_dkv, mla_kv_norm, w_mla_ukv, mla_kn_norm, mla_kp_norm)
    bp = x_prompt.shape[0]
    gla0 = jnp.zeros((bp, GLA_HEADS, GLA_DK, GLA_DV), F32)
    ffn0 = jnp.zeros((2, bp, CONV_W - 1, D_FF), F32)
    y_p, gla_p, ffn_p, ckv_p, kpe_p = _run_trunk(x_prompt, 0, gla0, ffn0, None, None, w)
    past = cache_mla_ckv.shape[1]
    y_s, gla_s, ffn_s, ckv_s, kpe_s = _run_trunk(x_sample, past, state_gla[0], state_ffn_conv,
                                                 cache_mla_ckv, cache_mla_kpe, w)
    return (y_p, y_s, gla_p, ffn_p, ckv_p, kpe_p, gla_s, ffn_s, ckv_s, kpe_s)
```

```python
import functools

import jax
import jax.numpy as jnp
import numpy as np
from jax import lax
from jax.experimental import pallas as pl
from jax.experimental.pallas import tpu as pltpu

D_MODEL = 1024
CHUNK = 64
GLA_HEADS = 4
GLA_DK = 128
GLA_DV = 256
GLA_GATE_RANK = 16
GLA_TAU = 16.0
MLA_HEADS = 8
NOPE_DIM = 128
ROPE_DIM = 64
V_DIM = 128
Q_RANK = 512
KV_RANK = 256
ROPE_THETA = 10000.0
D_FF = 2816
CONV_W = 3
EPS = 1e-6

LANES = 128
GLA_SUB = 8
GLA_GROUP = 4
QK_PAD = 256
MASK_VALUE = -1e30
VMEM_LIMIT = 56 * 1024 * 1024

F32 = jnp.float32
BF16 = jnp.bfloat16


def _dot(a, b):
    return jnp.dot(a, b, preferred_element_type=F32)


def _dot_nt(a, b):
    return lax.dot_general(a, b, (((1,), (1,)), ((), ())), preferred_element_type=F32)


def _dot_tn(a, b):
    return lax.dot_general(a, b, (((0,), (0,)), ((), ())), preferred_element_type=F32)


def _rms(x, g):
    return x * lax.rsqrt(jnp.mean(x * x, axis=-1, keepdims=True) + EPS) * g


def _const_spec(shape):
    zeros = (0,) * len(shape)
    return pl.BlockSpec(shape, lambda *_: zeros)


def _params(semantics):
    return pltpu.CompilerParams(dimension_semantics=semantics, vmem_limit_bytes=VMEM_LIMIT)


def _split3(x):
    hi = x.astype(BF16)
    r1 = x - hi.astype(F32)
    mid = r1.astype(BF16)
    lo = (r1 - mid.astype(F32)).astype(BF16)
    return hi, mid, lo


def _gla_intra_scores(k, b, q_c, k_c, b_c, ks, chunk):
    sub = GLA_SUB
    lane = lax.broadcasted_iota(jnp.int32, (sub, chunk), 1)
    row_in_blk = lax.broadcasted_iota(jnp.int32, (sub, chunk), 0)
    blocks = []
    for blk in range(chunk // sub):
        lo = blk * sub
        b_blk = b_c[lo:lo + sub, ks]
        q_blk = q_c[lo:lo + sub, ks]
        a_blk = jnp.zeros((sub, chunk), F32)
        for jj in range(sub):
            t = jnp.exp2(b_blk - b_c[lo + jj:lo + jj + 1, ks]) * (q_blk * k_c[lo + jj:lo + jj + 1, ks])
            a_blk = jnp.where(lane == lo + jj, jnp.sum(t, axis=-1, keepdims=True), a_blk)
        a_blk = jnp.where(lane <= lo + row_in_blk, a_blk, 0.0)
        if blk > 0:
            rho = b_c[lo - 1:lo, ks]
            qd = (q_blk * jnp.exp2(b_blk - rho)).astype(BF16)
            kd = (k[:lo] * jnp.exp2(rho - b[:lo])).astype(BF16)
            kd = jnp.concatenate([kd, jnp.zeros((chunk - lo, GLA_DK), BF16)], axis=0)
            a_blk = a_blk + _dot_nt(qd, kd)
        blocks.append(a_blk)
    return jnp.concatenate(blocks, axis=0)


def _gla_kernel(x_ref, s0_ref, g_ref, win_ref, wa2_ref, ba_ref, on_ref, wout_ref,
                y_ref, s_ref, q_s, k_s, v_s, r_s, b_s, o_s, st_s, qc_s, kc_s, bc_s,
                *, nb, tb, chunk):
    step = pl.program_id(1)
    heads, dk, dv = GLA_HEADS, GLA_DK, GLA_DV
    m = nb * tb
    units_per_seq = tb // chunk

    @pl.when(step == 0)
    def _load_state():
        for n in range(nb):
            for hd in range(heads):
                st_s[n, hd] = s0_ref[n, hd].T

    x = x_ref[...].reshape(m, D_MODEL)
    h = _rms(x, g_ref[...]).astype(BF16)
    z = _dot(h, win_ref[...])
    nk = heads * dk
    nv = heads * dv
    q_s[...] = z[:, :nk] * (dk ** -0.5)
    k_s[...] = z[:, nk:2 * nk]
    v_s[...] = z[:, 2 * nk:2 * nk + nv].astype(BF16)
    r_s[...] = z[:, 2 * nk + nv:2 * nk + 2 * nv]
    gate_lo = z[:, 2 * nk + 2 * nv:].astype(BF16)
    xg = _dot(gate_lo, wa2_ref[...]) + ba_ref[...]
    log_a = (jnp.minimum(xg, 0.0) - jnp.log1p(jnp.exp(-jnp.abs(xg)))) * (1.0 / GLA_TAU)

    tri_r = lax.broadcasted_iota(jnp.int32, (chunk, chunk), 0)
    tri_c = lax.broadcasted_iota(jnp.int32, (chunk, chunk), 1)
    tril = (tri_c <= tri_r).astype(BF16)
    log2_e = float(np.log2(np.e))
    for u in range(m // chunk):
        hi, mid, lo = _split3(log_a[u * chunk:(u + 1) * chunk])
        b_s[u * chunk:(u + 1) * chunk, :] = (_dot(tril, hi) + _dot(tril, mid) + _dot(tril, lo)) * log2_e

    def unit(u, slot):
        r0 = pl.multiple_of(u * chunk, chunk)
        n = u // units_per_seq
        rows = pl.ds(r0, chunk)
        q_c, k_c, b_c = qc_s.at[slot], kc_s.at[slot], bc_s.at[slot]
        q_c[...] = q_s[rows, :]
        k_c[...] = k_s[rows, :]
        b_c[...] = b_s[rows, :]
        for hd in range(heads):
            ks = slice(hd * dk, (hd + 1) * dk)
            vs = slice(hd * dv, (hd + 1) * dv)
            q = q_c[:, ks]
            k = k_c[:, ks]
            b = b_c[:, ks]
            v = v_s[rows, vs]
            st = st_s[n, hd]
            b_last = b_c[chunk - 1:chunk, ks]
            o = _dot_nt((q * jnp.exp2(b)).astype(BF16), st.astype(BF16))
            scores = _gla_intra_scores(k, b, q_c, k_c, b_c, ks, chunk)
            o = o + _dot(scores.astype(BF16), v)
            k_dec = (k * jnp.exp2(b_last - b)).astype(BF16)
            st_s[n, hd] = st * jnp.exp2(b_last) + _dot_tn(v, k_dec)
            o = _rms(o, on_ref[...])
            r = r_s[rows, vs]
            o_s[rows, vs] = (o * (r * jax.nn.sigmoid(r))).astype(BF16)

    def group_body(g, carry):
        for slot in range(GLA_GROUP):
            unit(g * GLA_GROUP + slot, slot)
        return carry

    assert (m // chunk) % GLA_GROUP == 0
    lax.fori_loop(0, m // chunk // GLA_GROUP, group_body, 0)

    y = _dot(o_s[...], wout_ref[...]) + x
    y_ref[...] = y.reshape(nb, tb, D_MODEL)

    @pl.when(step == pl.num_programs(1) - 1)
    def _store_state():
        for n in range(nb):
            for hd in range(heads):
                s_ref[n, hd] = st_s[n, hd].T


def _gla_layer(x, s0, w, *, nb, tb, chunk):
    bsz, seq, _ = x.shape
    m = nb * tb
    nk, nv = GLA_HEADS * GLA_DK, GLA_HEADS * GLA_DV
    kern = functools.partial(_gla_kernel, nb=nb, tb=tb, chunk=chunk)
    state_spec = pl.BlockSpec((nb, GLA_HEADS, GLA_DK, GLA_DV), lambda b, i: (b, 0, 0, 0))
    x_spec = pl.BlockSpec((nb, tb, D_MODEL), lambda b, i: (b, i, 0))
    return pl.pallas_call(
        kern,
        grid=(bsz // nb, seq // tb),
        in_specs=[x_spec, state_spec,
                  _const_spec((1, D_MODEL)), _const_spec(w["gla_in"].shape),
                  _const_spec(w["gla_a2"].shape), _const_spec((1, nk)),
                  _const_spec((1, GLA_DV)), _const_spec(w["gla_out"].shape)],
        out_specs=[x_spec, state_spec],
        out_shape=[jax.ShapeDtypeStruct(x.shape, F32), jax.ShapeDtypeStruct(s0.shape, F32)],
        scratch_shapes=[pltpu.VMEM((m, nk), F32), pltpu.VMEM((m, nk), F32),
                        pltpu.VMEM((m, nv), BF16), pltpu.VMEM((m, nv), F32),
                        pltpu.VMEM((m, nk), F32), pltpu.VMEM((m, nv), BF16),
                        pltpu.VMEM((nb, GLA_HEADS, GLA_DV, GLA_DK), F32),
                        pltpu.VMEM((GLA_GROUP, chunk, nk), F32),
                        pltpu.VMEM((GLA_GROUP, chunk, nk), F32),
                        pltpu.VMEM((GLA_GROUP, chunk, nk), F32)],
        compiler_params=_params(("arbitrary", "arbitrary")),
        name="gla_layer",
    )(x, s0, w["norm_mix0"], w["gla_in"], w["gla_a2"], w["gla_ba"], w["gla_on"], w["gla_out"])


def _ffn_kernel(x_ref, prev_ref, g_ref, win_ref, cw_ref, cb_ref, wout_ref,
                y_ref, tail_ref, carry_s, *, nb, tb):
    step = pl.program_id(1)
    m = nb * tb

    @pl.when(step == 0)
    def _load_carry():
        carry_s[...] = prev_ref[...]

    x = x_ref[...].reshape(m, D_MODEL)
    h = _rms(x, g_ref[...]).astype(BF16)
    z = _dot(h, win_ref[...])
    gate = z[:, :D_FF]
    up = z[:, D_FF:]
    row = lax.broadcasted_iota(jnp.int32, (tb, D_FF), 0)
    w0, w1, w2 = cw_ref[0:1, :], cw_ref[1:2, :], cw_ref[2:3, :]
    convs = []
    for n in range(nb):
        g_n = gate[n * tb:(n + 1) * tb]
        c0 = carry_s[n, 0:1, :]
        c1 = carry_s[n, 1:2, :]
        back1 = jnp.where(row == 0, c1, pltpu.roll(g_n, 1, 0))
        back2 = jnp.where(row == 0, c0, jnp.where(row == 1, c1, pltpu.roll(g_n, 2, 0)))
        convs.append(cb_ref[...] + ((back2 * w0 + back1 * w1) + g_n * w2))
        carry_s[n] = g_n[tb - (CONV_W - 1):tb]
    conv = jnp.concatenate(convs, axis=0) if nb > 1 else convs[0]
    gelu = 0.5 * conv * (1.0 + lax.erf(conv * (0.5 ** 0.5)))
    act = (gelu * up).astype(BF16)
    y = _dot(act, wout_ref[...]) + x
    y_ref[...] = y.reshape(nb, tb, D_MODEL)
    tail_ref[...] = carry_s[...]


def _ffn_layer(x, prev, w, layer, *, nb, tb):
    bsz, seq, _ = x.shape
    kern = functools.partial(_ffn_kernel, nb=nb, tb=tb)
    x_spec = pl.BlockSpec((nb, tb, D_MODEL), lambda b, i: (b, i, 0))
    tail_spec = pl.BlockSpec((nb, CONV_W - 1, D_FF), lambda b, i: (b, 0, 0))
    return pl.pallas_call(
        kern,
        grid=(bsz // nb, seq // tb),
        in_specs=[x_spec, tail_spec, _const_spec((1, D_MODEL)),
                  _const_spec((D_MODEL, 2 * D_FF)), _const_spec((CONV_W, D_FF)),
                  _const_spec((1, D_FF)), _const_spec((D_FF, D_MODEL))],
        out_specs=[x_spec, tail_spec],
        out_shape=[jax.ShapeDtypeStruct(x.shape, F32), jax.ShapeDtypeStruct(prev.shape, F32)],
        scratch_shapes=[pltpu.VMEM((nb, CONV_W - 1, D_FF), F32)],
        compiler_params=_params(("arbitrary", "arbitrary")),
        name=f"conv_glu_{layer}",
    )(x, prev, w["norm_ffn"][layer], w["ffn_in"][layer], w["ffn_cw"][layer],
      w["ffn_cb"][layer], w["ffn_out"][layer])


def _rope_pair(a, gain, cs):
    rs = lax.rsqrt(jnp.sum(a * a, axis=-1, keepdims=True) * (1.0 / LANES) + EPS)
    t = a * rs * gain * cs
    return t + pltpu.roll(t, ROPE_DIM, 1)


def _mla_prep_kernel(x_ref, cs_ref, gkv_ref, gq_ref, wdkv_ref, kvn_ref, kpg_ref,
                     wdq_ref, qn_ref, wuq_ref, qnn_ref, qpg_ref,
                     ckv_ref, kpe_ref, q_ref):
    x = x_ref[...]
    xn = x * lax.rsqrt(jnp.mean(x * x, axis=-1, keepdims=True) + EPS)
    cs = cs_ref[...]
    zk = _dot((xn * gkv_ref[...]).astype(BF16), wdkv_ref[...])
    ckv_ref[...] = _rms(zk[:, :KV_RANK], kvn_ref[...])
    kpe_ref[...] = _rope_pair(zk[:, KV_RANK:], kpg_ref[...], cs)[:, :ROPE_DIM]

    cq = _rms(_dot((xn * gq_ref[...]).astype(BF16), wdq_ref[...]), qn_ref[...]).astype(BF16)
    zq = _dot(cq, wuq_ref[...])
    scale = (NOPE_DIM + ROPE_DIM) ** -0.5 * float(np.log2(np.e))
    for hd in range(MLA_HEADS):
        lo = hd * QK_PAD
        q_nope = _rms(zq[:, lo:lo + NOPE_DIM], qnn_ref[...])
        q_pe = _rope_pair(zq[:, lo + NOPE_DIM:lo + QK_PAD], qpg_ref[...], cs)
        q_ref[:, lo:lo + NOPE_DIM] = (q_nope * scale).astype(BF16)
        q_ref[:, lo + NOPE_DIM:lo + QK_PAD] = (q_pe * scale).astype(BF16)


def _mla_prep(x2d, cs, w, *, tm, cs_blocks):
    rows = x2d.shape[0]
    row_spec = lambda width: pl.BlockSpec((tm, width), lambda r: (r, 0))
    return pl.pallas_call(
        _mla_prep_kernel,
        grid=(rows // tm,),
        in_specs=[row_spec(D_MODEL), pl.BlockSpec((tm, LANES), lambda r: (r % cs_blocks, 0)),
                  _const_spec((1, D_MODEL)), _const_spec((1, D_MODEL)),
                  _const_spec(w["mla_dkv"].shape), _const_spec((1, KV_RANK)), _const_spec((1, LANES)),
                  _const_spec(w["mla_dq"].shape), _const_spec((1, Q_RANK)),
                  _const_spec(w["mla_uq"].shape), _const_spec((1, NOPE_DIM)), _const_spec((1, LANES))],
        out_specs=[row_spec(KV_RANK), row_spec(ROPE_DIM), row_spec(MLA_HEADS * QK_PAD)],
        out_shape=[jax.ShapeDtypeStruct((rows, KV_RANK), F32),
                   jax.ShapeDtypeStruct((rows, ROPE_DIM), F32),
                   jax.ShapeDtypeStruct((rows, MLA_HEADS * QK_PAD), BF16)],
        compiler_params=_params(("arbitrary",)),
        name="mla_prep",
    )(x2d, cs, w["kv_stream_norm"], w["norm_mix1"], w["mla_dkv"], w["mla_kvn"], w["mla_kpg"],
      w["mla_dq"], w["mla_qn"], w["mla_uq"], w["mla_qnn"], w["mla_qpg"])


def _upproj_kernel(ckv_ref, kpe_ref, wuk_ref, wuvt_ref, knn_ref, k_ref, vt_ref):
    ckv = ckv_ref[...].astype(BF16)
    z = _dot(ckv, wuk_ref[...])
    kpe = kpe_ref[...].astype(BF16)
    for hd in range(MLA_HEADS):
        lo = hd * QK_PAD
        k_ref[:, lo:lo + NOPE_DIM] = _rms(z[:, hd * NOPE_DIM:(hd + 1) * NOPE_DIM],
                                          knn_ref[...]).astype(BF16)
        k_ref[:, lo + NOPE_DIM:lo + QK_PAD] = kpe
    vt_ref[...] = _dot_nt(wuvt_ref[...], ckv).astype(BF16)


def _mla_upproj(ckv, kpe, w, *, tm):
    bsz, t_pad, _ = ckv.shape
    row_spec = lambda width: pl.BlockSpec((None, tm, width), lambda b, t: (b, t, 0))
    return pl.pallas_call(
        _upproj_kernel,
        grid=(bsz, t_pad // tm),
        in_specs=[row_spec(KV_RANK), row_spec(LANES), _const_spec(w["mla_uk"].shape),
                  _const_spec(w["mla_uvt"].shape), _const_spec((1, NOPE_DIM))],
        out_specs=[row_spec(MLA_HEADS * QK_PAD),
                   pl.BlockSpec((None, MLA_HEADS * V_DIM, tm), lambda b, t: (b, 0, t))],
        out_shape=[jax.ShapeDtypeStruct((bsz, t_pad, MLA_HEADS * QK_PAD), BF16),
                   jax.ShapeDtypeStruct((bsz, MLA_HEADS * V_DIM, t_pad), BF16)],
        compiler_params=_params(("arbitrary", "arbitrary")),
        name="mla_upproj",
    )(ckv, kpe, w["mla_uk"], w["mla_uvt"], w["mla_knn"])


def _last_kv_block(qi, *, tq, tk, q_off, t_valid):
    last_q = q_off + (qi + 1) * tq - 1
    chunk_end = (last_q // CHUNK + 1) * CHUNK
    visible = min(t_valid, chunk_end) if isinstance(qi, int) else jnp.minimum(t_valid, chunk_end)
    return (visible - 1) // tk


def _attn_kernel(qi_ref, ki_ref, q_ref, k_ref, vt_ref, x_ref, wo_ref, y_ref, m_s, l_s, acc_s, o_s,
                 *, tq, tk, q_off, t_valid):
    qi = qi_ref[pl.program_id(1)]
    ki = ki_ref[pl.program_id(1)]

    @pl.when(ki == 0)
    def _init():
        m_s[...] = jnp.full(m_s.shape, MASK_VALUE, F32)
        l_s[...] = jnp.zeros(l_s.shape, F32)
        acc_s[...] = jnp.zeros(acc_s.shape, F32)

    def scores(hd):
        qk = slice(hd * QK_PAD, (hd + 1) * QK_PAD)
        return _dot_nt(k_ref[:, qk], q_ref[:, qk])

    def step(masked):
        if masked:
            k_pos = ki * tk + lax.broadcasted_iota(jnp.int32, (tk, tq), 0)
            q_pos = q_off + qi * tq + lax.broadcasted_iota(jnp.int32, (tk, tq), 1)
            shift = CHUNK.bit_length() - 1
            visible = ((k_pos >> shift) <= (q_pos >> shift)) & (k_pos < t_valid)
        s_next = scores(0)
        for hd in range(MLA_HEADS):
            row = slice(hd, hd + 1)
            s = s_next
            if hd + 1 < MLA_HEADS:
                s_next = scores(hd + 1)
            if masked:
                s = jnp.where(visible, s, MASK_VALUE)
            m_old = m_s[row, :]
            m_new = jnp.maximum(m_old, jnp.max(s, axis=0, keepdims=True))
            p = jnp.exp2(s - m_new)
            alpha = jnp.exp2(m_old - m_new)
            l_s[row, :] = alpha * l_s[row, :] + jnp.sum(p, axis=0, keepdims=True)
            acc_s[hd] = alpha * acc_s[hd] + _dot(vt_ref[hd * V_DIM:(hd + 1) * V_DIM, :],
                                                 p.astype(BF16))
            m_s[row, :] = m_new

    first_q = q_off + qi * tq
    block_end = (ki + 1) * tk
    needs_mask = (block_end > (first_q // CHUNK + 1) * CHUNK) | (block_end > t_valid)
    pl.when(needs_mask)(functools.partial(step, True))
    pl.when(jnp.logical_not(needs_mask))(functools.partial(step, False))

    @pl.when(ki == _last_kv_block(qi, tq=tq, tk=tk, q_off=q_off, t_valid=t_valid))
    def _finish():
        for hd in range(MLA_HEADS):
            o_s[hd * V_DIM:(hd + 1) * V_DIM, :] = (acc_s[hd] / l_s[hd:hd + 1, :]).astype(BF16)
        y_ref[...] = _dot_tn(o_s[...], wo_ref[...]) + x_ref[...]


def _mla_attention(q, k, vt, x, w_out, *, tq, tk, q_off, t_valid):
    bsz, seq, _ = x.shape
    pairs = [(i, j) for i in range(seq // tq)
             for j in range(_last_kv_block(i, tq=tq, tk=tk, q_off=q_off, t_valid=t_valid) + 1)]
    assert (max(j for _, j in pairs) + 1) * tk <= k.shape[1]
    qi_tab = jnp.asarray([i for i, _ in pairs], jnp.int32)
    ki_tab = jnp.asarray([j for _, j in pairs], jnp.int32)
    kern = functools.partial(_attn_kernel, tq=tq, tk=tk, q_off=q_off, t_valid=t_valid)
    q_spec = lambda width: pl.BlockSpec((None, tq, width), lambda b, s, qt, kt: (b, qt[s], 0))
    grid_spec = pltpu.PrefetchScalarGridSpec(
        num_scalar_prefetch=2,
        grid=(bsz, len(pairs)),
        in_specs=[q_spec(MLA_HEADS * QK_PAD),
                  pl.BlockSpec((None, tk, MLA_HEADS * QK_PAD), lambda b, s, qt, kt: (b, kt[s], 0)),
                  pl.BlockSpec((None, MLA_HEADS * V_DIM, tk), lambda b, s, qt, kt: (b, 0, kt[s])),
                  q_spec(D_MODEL),
                  pl.BlockSpec(w_out.shape, lambda b, s, qt, kt: (0, 0))],
        out_specs=q_spec(D_MODEL),
        scratch_shapes=[pltpu.VMEM((MLA_HEADS, tq), F32), pltpu.VMEM((MLA_HEADS, tq), F32),
                        pltpu.VMEM((MLA_HEADS, V_DIM, tq), F32),
                        pltpu.VMEM((MLA_HEADS * V_DIM, tq), BF16)])
    return pl.pallas_call(
        kern,
        grid_spec=grid_spec,
        out_shape=jax.ShapeDtypeStruct(x.shape, F32),
        compiler_params=_params(("arbitrary", "arbitrary")),
        name="mla_attention",
    )(qi_tab, ki_tab, q, k, vt, x, w_out)


def _decode_kernel(cckv_ref, ckpe_ref, nckv_ref, nkpe_ref, qbd_ref, qpe_ref, x_ref,
                   wuk_ref, wuvt_ref, knn_ref, wo_ref, y_ref, m_s, l_s, acc_s, o_s,
                   *, seq, tk, q_off, n_past):
    ki = pl.program_id(1)
    cols = MLA_HEADS * seq

    @pl.when(ki == 0)
    def _init():
        m_s[...] = jnp.full(m_s.shape, MASK_VALUE, F32)
        l_s[...] = jnp.zeros(l_s.shape, F32)
        acc_s[...] = jnp.zeros(acc_s.shape, F32)

    def attend(ckv, kpe, k_pos0, masked):
        rows = ckv.shape[0]
        c = ckv.astype(BF16)
        z = _dot(c, wuk_ref[...])
        k_nope = jnp.concatenate(
            [_rms(z[:, hd * NOPE_DIM:(hd + 1) * NOPE_DIM], knn_ref[...]).astype(BF16)
             for hd in range(MLA_HEADS)], axis=1)
        vt = _dot_nt(wuvt_ref[...], c).astype(BF16)
        s = _dot(k_nope, qbd_ref[...]) + _dot(kpe.astype(BF16), qpe_ref[...])
        if masked:
            k_pos = k_pos0 + lax.broadcasted_iota(jnp.int32, (rows, cols), 0)
            q_pos = q_off + (lax.broadcasted_iota(jnp.int32, (rows, cols), 1) & (seq - 1))
            shift = CHUNK.bit_length() - 1
            s = jnp.where((k_pos >> shift) <= (q_pos >> shift), s, MASK_VALUE)
        m_old = m_s[...]
        m_new = jnp.maximum(m_old, jnp.max(s, axis=0, keepdims=True))
        p = jnp.exp2(s - m_new)
        alpha = jnp.exp2(m_old - m_new)
        l_s[...] = alpha * l_s[...] + jnp.sum(p, axis=0, keepdims=True)
        acc_s[...] = alpha * acc_s[...] + _dot(vt, p.astype(BF16))
        m_s[...] = m_new

    @pl.when(ki < n_past)
    def _cached():
        attend(cckv_ref[...], ckpe_ref[...], ki * tk, False)

    @pl.when(ki == n_past)
    def _new_and_finish():
        attend(nckv_ref[...], nkpe_ref[...], q_off, True)
        inv_l = 1.0 / l_s[...]
        q_idx = lax.broadcasted_iota(jnp.int32, (seq, cols), 0)
        col = lax.broadcasted_iota(jnp.int32, (seq, cols), 1)
        for hd in range(MLA_HEADS):
            vs = slice(hd * V_DIM, (hd + 1) * V_DIM)
            o_t = (acc_s[vs, :] * inv_l).astype(BF16)
            pick = (col == hd * seq + q_idx).astype(BF16)
            o_s[:, vs] = _dot_nt(pick, o_t).astype(BF16)
        y_ref[...] = _dot(o_s[...], wo_ref[...]) + x_ref[...]


def _mla_decode(cache_ckv, cache_kpe, new_ckv, new_kpe, qbd, qpe, x, w, *, tk, q_off):
    bsz, seq, _ = x.shape
    past = cache_ckv.shape[1]
    cols = MLA_HEADS * seq
    assert past % tk == 0 and q_off == past and seq & (seq - 1) == 0 and cols % LANES == 0
    n_past = past // tk
    kern = functools.partial(_decode_kernel, seq=seq, tk=tk, q_off=q_off, n_past=n_past)
    cache_spec = lambda width: pl.BlockSpec(
        (None, tk, width), lambda b, j: (b, jnp.minimum(j, n_past - 1), 0))
    batch_spec = lambda r, width: pl.BlockSpec((None, r, width), lambda b, j: (b, 0, 0))
    return pl.pallas_call(
        kern,
        grid=(bsz, n_past + 1),
        in_specs=[cache_spec(KV_RANK), cache_spec(ROPE_DIM),
                  batch_spec(seq, KV_RANK), batch_spec(seq, ROPE_DIM),
                  batch_spec(MLA_HEADS * NOPE_DIM, cols), batch_spec(ROPE_DIM, cols),
                  batch_spec(seq, D_MODEL),
                  _const_spec(w["mla_uk"].shape), _const_spec(w["mla_uvt"].shape),
                  _const_spec((1, NOPE_DIM)), _const_spec(w["mla_out"].shape)],
        out_specs=batch_spec(seq, D_MODEL),
        out_shape=jax.ShapeDtypeStruct(x.shape, F32),
        scratch_shapes=[pltpu.VMEM((1, cols), F32), pltpu.VMEM((1, cols), F32),
                        pltpu.VMEM((MLA_HEADS * V_DIM, cols), F32),
                        pltpu.VMEM((seq, MLA_HEADS * V_DIM), BF16)],
        compiler_params=_params(("arbitrary", "arbitrary")),
        name="mla_decode",
    )(cache_ckv, cache_kpe, new_ckv, new_kpe, qbd, qpe, x,
      w["mla_uk"], w["mla_uvt"], w["mla_knn"], w["mla_out"])


def _decode_queries(q, bsz, seq):
    q4 = q.reshape(bsz, seq, MLA_HEADS, QK_PAD)
    eye = jnp.eye(MLA_HEADS, dtype=q.dtype)
    qbd = jnp.einsum("bihd,hg->bhdgi", q4[..., :NOPE_DIM], eye)
    qbd = qbd.reshape(bsz, MLA_HEADS * NOPE_DIM, MLA_HEADS * seq)
    qpe = q4[..., NOPE_DIM:NOPE_DIM + ROPE_DIM].transpose(0, 3, 2, 1)
    return qbd, qpe.reshape(bsz, ROPE_DIM, MLA_HEADS * seq)


def _rotate_half_cols(w):
    half = ROPE_DIM // 2
    return jnp.concatenate([-w[..., half:], w[..., :half]], axis=-1)


def _swap_halves(g):
    half = ROPE_DIM // 2
    return jnp.concatenate([g[..., half:], g[..., :half]], axis=-1)


def _prepare_weights(norm_mix, norm_ffn, w_ffn_in, ffn_conv_w, ffn_conv_b, w_ffn_out,
                     w_gla_in, w_gla_a1, w_gla_a2, b_gla_a, gla_out_norm, w_gla_out,
                     w_mla_dq, mla_q_norm, w_mla_uq, mla_qn_norm, mla_qp_norm, w_mla_out,
                     kv_stream_norm, w_mla_dkv, mla_kv_norm, w_mla_ukv, mla_kn_norm, mla_kp_norm):
    row = lambda g: g.reshape(1, -1).astype(F32)
    pad_rank = LANES - GLA_GATE_RANK
    gla_in = jnp.concatenate([w_gla_in[0], jnp.pad(w_gla_a1[0], ((0, 0), (0, pad_rank)))], axis=1)
    gla_a2 = jnp.pad(w_gla_a2[0], ((0, pad_rank), (0, 0)))

    dkv_pe = w_mla_dkv[:, KV_RANK:]
    mla_dkv = jnp.concatenate([w_mla_dkv[:, :KV_RANK], dkv_pe, _rotate_half_cols(dkv_pe)], axis=1)

    uq = w_mla_uq[0].reshape(Q_RANK, MLA_HEADS, NOPE_DIM + ROPE_DIM)
    uq_pe = uq[..., NOPE_DIM:]
    mla_uq = jnp.concatenate([uq[..., :NOPE_DIM], uq_pe, _rotate_half_cols(uq_pe)], axis=-1)
    mla_uq = mla_uq.reshape(Q_RANK, MLA_HEADS * QK_PAD)

    ukv = w_mla_ukv.reshape(KV_RANK, MLA_HEADS, NOPE_DIM + V_DIM)
    mla_uk = ukv[..., :NOPE_DIM].reshape(KV_RANK, MLA_HEADS * NOPE_DIM)
    mla_uvt = ukv[..., NOPE_DIM:].reshape(KV_RANK, MLA_HEADS * V_DIM).T

    pair_gain = lambda g: row(jnp.concatenate([g, _swap_halves(g)], axis=-1))
    return {
        "norm_mix0": row(norm_mix[0]), "norm_mix1": row(norm_mix[1]),
        "norm_ffn": [row(norm_ffn[l]) for l in range(2)],
        "ffn_in": [w_ffn_in[l].astype(BF16) for l in range(2)],
        "ffn_cw": [ffn_conv_w[l] for l in range(2)],
        "ffn_cb": [row(ffn_conv_b[l]) for l in range(2)],
        "ffn_out": [w_ffn_out[l].astype(BF16) for l in range(2)],
        "gla_in": gla_in.astype(BF16), "gla_a2": gla_a2.astype(BF16), "gla_ba": row(b_gla_a[0]),
        "gla_on": row(gla_out_norm[0]), "gla_out": w_gla_out[0].astype(BF16),
        "kv_stream_norm": row(kv_stream_norm), "mla_dkv": mla_dkv.astype(BF16),
        "mla_kvn": row(mla_kv_norm), "mla_kpg": pair_gain(mla_kp_norm),
        "mla_dq": w_mla_dq[0].astype(BF16), "mla_qn": row(mla_q_norm[0]),
        "mla_uq": mla_uq.astype(BF16), "mla_qnn": row(mla_qn_norm[0]),
        "mla_qpg": pair_gain(mla_qp_norm[0]),
        "mla_uk": mla_uk.astype(BF16), "mla_uvt": mla_uvt.astype(BF16),
        "mla_knn": row(mla_kn_norm),
        "mla_out": w_mla_out[0].astype(BF16),
    }


def _rope_table(pos0, seq):
    half = ROPE_DIM // 2
    inv = ROPE_THETA ** (-jnp.arange(half, dtype=F32) * 2.0 / ROPE_DIM)
    ang = (pos0 + jnp.arange(seq, dtype=jnp.int32)).astype(F32)[:, None] * inv[None, :]
    cos, sin = jnp.cos(ang), jnp.sin(ang)
    return jnp.concatenate([cos, cos, sin, sin], axis=-1)


def _tiles(bsz, seq):
    if seq <= 256:
        return dict(nb=bsz, tb=seq, tb_gla=seq, tm=bsz * seq, tq=seq, tk=512)
    return dict(nb=1, tb=256, tb_gla=256, tm=512, tq=512, tk=512)


def _run_trunk(x, pos0, gla_s0, ffn_prev, past_ckv, past_kpe, w):
    bsz, seq, _ = x.shape
    plan = _tiles(bsz, seq)
    nb, tb, tm, tq = plan["nb"], plan["tb"], plan["tm"], plan["tq"]
    chunk = min(CHUNK, seq)

    x1, gla_s = _gla_layer(x, gla_s0, w, nb=nb, tb=plan["tb_gla"], chunk=chunk)
    x2, tail0 = _ffn_layer(x1, ffn_prev[0], w, 0, nb=nb, tb=tb)

    cs = _rope_table(pos0, seq)
    if tm > seq:
        cs = jnp.tile(cs, (tm // seq, 1))
    ckv, kpe, q = _mla_prep(x2.reshape(bsz * seq, D_MODEL), cs, w, tm=tm,
                            cs_blocks=cs.shape[0] // tm)
    ckv = ckv.reshape(bsz, seq, KV_RANK)
    kpe = kpe.reshape(bsz, seq, ROPE_DIM)
    tk = plan["tk"]
    if past_ckv is None:
        assert seq % tk == 0
        kpe_lanes = jnp.pad(kpe, ((0, 0), (0, 0), (0, LANES - ROPE_DIM)))
        k, vt = _mla_upproj(ckv, kpe_lanes, w, tm=tk)
        x3 = _mla_attention(q.reshape(bsz, seq, MLA_HEADS * QK_PAD), k, vt, x2, w["mla_out"],
                            tq=tq, tk=tk, q_off=pos0, t_valid=seq)
    else:
        qbd, qpe = _decode_queries(q, bsz, seq)
        x3 = _mla_decode(past_ckv, past_kpe, ckv, kpe, qbd, qpe, x2, w, tk=tk, q_off=pos0)

    x4, tail1 = _ffn_layer(x3, ffn_prev[1], w, 1, nb=nb, tb=tb)
    return x4, gla_s[None], jnp.stack([tail0, tail1]), ckv, kpe


def kernel(x_prompt, x_sample, state_gla, state_ffn_conv, cache_mla_ckv, cache_mla_kpe, norm_mix, norm_ffn, w_ffn_in, ffn_conv_w, ffn_conv_b, w_ffn_out, w_gla_in, w_gla_a1, w_gla_a2, b_gla_a, gla_out_norm, w_gla_out, w_mla_dq, mla_q_norm, w_mla_uq, mla_qn_norm, mla_qp_norm, w_mla_out, kv_stream_norm, w_mla_dkv, mla_kv_norm, w_mla_ukv, mla_kn_norm, mla_kp_norm):
    assert norm_mix.shape[0] == 2 and w_gla_in.shape[0] == 1 and w_mla_dq.shape[0] == 1
    w = _prepare_weights(norm_mix, norm_ffn, w_ffn_in, ffn_conv_w, ffn_conv_b, w_ffn_out,
                         w_gla_in, w_gla_a1, w_gla_a2, b_gla_a, gla_out_norm, w_gla_out,
                         w_mla_dq, mla_q_norm, w_mla_uq, mla_qn_norm, mla_qp_norm, w_mla_out,
                         kv_stream_norm, w_mla_dkv, mla_kv_norm, w_mla_ukv, mla_kn_norm, mla_kp_norm)
    bp = x_prompt.shape[0]
    gla0 = jnp.zeros((bp, GLA_HEADS, GLA_DK, GLA_DV), F32)
    ffn0 = jnp.zeros((2, bp, CONV_W - 1, D_FF), F32)
    y_p, gla_p, ffn_p, ckv_p, kpe_p = _run_trunk(x_prompt, 0, gla0, ffn0, None, None, w)
    past = cache_mla_ckv.shape[1]
    y_s, gla_s, ffn_s, ckv_s, kpe_s = _run_trunk(x_sample, past, state_gla[0], state_ffn_conv,
                                                 cache_mla_ckv, cache_mla_kpe, w)
    return (y_p, y_s, gla_p, ffn_p, ckv_p, kpe_p, gla_s, ffn_s, ckv_s, kpe_s)
```
